```python
import math
import jax, jax.numpy as jnp
from jax import lax
import numpy as np

D_MODEL = 1024
BATCH = 8
SEQ = 4096
DEPTH = 2

CONV_CH = D_MODEL // 2
CONV_GROUPS = 8
CONV_WIDTH = 31
MLA_HEADS = 8
QK_NOPE = 64
QK_ROPE = 32
V_HEAD = 64
Q_LORA = 384
KV_LORA = 256
ROPE_THETA = 10000.0
Q_BLOCK = 128
MIX_WIDTH = CONV_CH + MLA_HEADS * V_HEAD
IN_COLS = 2 * CONV_CH + Q_LORA + KV_LORA + QK_ROPE
POOL_WINDOWS = (2, 4, 8, 16)
POOL_GROUPS = len(POOL_WINDOWS)
POOL_GROUP = D_MODEL // POOL_GROUPS
D_FF = 2816
FFN_CONV = 3
EPS = 1e-6
N_EVEN = (DEPTH + 1) // 2
N_ODD = DEPTH // 2

kernel_name = "hybrid_conv_mla_pool_convffn"


def rmsnorm(x, g):
    xf = x.astype(jnp.float32)
    y = xf * lax.rsqrt(jnp.mean(xf * xf, axis=-1, keepdims=True) + EPS)
    return (y * g.astype(jnp.float32)).astype(x.dtype)


def layernorm(x, g, b):
    xf = x.astype(jnp.float32)
    mu = jnp.mean(xf, axis=-1, keepdims=True)
    var = jnp.mean(jnp.square(xf - mu), axis=-1, keepdims=True)
    y = (xf - mu) * lax.rsqrt(var + EPS)
    return (y * g.astype(jnp.float32) + b.astype(jnp.float32)).astype(x.dtype)


def causal_dwconv(x, w, b):
    K, C = w.shape
    y = lax.conv_general_dilated(
        x, w[:, None, :].astype(x.dtype), window_strides=(1,), padding=[(K - 1, 0)],
        dimension_numbers=("NWC", "WIO", "NWC"), feature_group_count=C)
    return y + b.astype(x.dtype)


def rope_tables(positions):
    inv_freq = 1.0 / (ROPE_THETA ** (jnp.arange(0, QK_ROPE, 2, dtype=jnp.float32) / QK_ROPE))
    ang = positions.astype(jnp.float32)[..., None] * inv_freq
    return jnp.cos(ang), jnp.sin(ang)


def apply_rope(x, cos, sin):
    xf = x.astype(jnp.float32)
    x1, x2 = jnp.split(xf, 2, axis=-1)
    return jnp.concatenate([x1 * cos - x2 * sin, x1 * sin + x2 * cos], axis=-1).astype(x.dtype)


def mla_attention(qn, qr, kn, kr, v):
    S = qn.shape[1]
    scale = 1.0 / math.sqrt(QK_NOPE + QK_ROPE)
    outs = []
    for i in range(S // Q_BLOCK):
        q0, k_end = i * Q_BLOCK, (i + 1) * Q_BLOCK
        s = (jnp.einsum("bqhd,bkhd->bhqk", qn[:, q0:k_end], kn[:, :k_end])
             + jnp.einsum("bqhr,bkr->bhqk", qr[:, q0:k_end], kr[:, :k_end]))
        s = s.astype(jnp.float32) * scale
        mask = (q0 + jnp.arange(Q_BLOCK))[:, None] >= jnp.arange(k_end)[None, :]
        s = jnp.where(mask, s, jnp.finfo(jnp.float32).min)
        p = jax.nn.softmax(s, axis=-1).astype(v.dtype)
        outs.append(jnp.einsum("bhqk,bkhd->bqhd", p, v[:, :k_end]))
    o = jnp.concatenate(outs, axis=1)
    return o.reshape(o.shape[0], S, MLA_HEADS * V_HEAD)


def conv_mla_mixer(h, cos, sin, w_in, conv_w, conv_b, conv_ln_g, conv_ln_b,
                   q_norm_g, w_uq, kv_norm_g, w_ukv, w_out):
    B, S, _ = h.shape
    proj = h @ w_in
    c0 = 2 * CONV_CH
    a, g = proj[..., :CONV_CH], proj[..., CONV_CH:c0]
    u = a * jax.nn.sigmoid(g)
    u = causal_dwconv(u, conv_w, conv_b)
    u = jax.nn.silu(layernorm(u, conv_ln_g, conv_ln_b))
    cq = proj[..., c0:c0 + Q_LORA]
    ckv = proj[..., c0 + Q_LORA:c0 + Q_LORA + KV_LORA]
    kr = proj[..., c0 + Q_LORA + KV_LORA:]
    q = (rmsnorm(cq, q_norm_g) @ w_uq).reshape(B, S, MLA_HEADS, QK_NOPE + QK_ROPE)
    qn, qr = q[..., :QK_NOPE], q[..., QK_NOPE:]
    qr = apply_rope(qr, cos[:, :, None, :], sin[:, :, None, :])
    kr = apply_rope(kr, cos, sin)
    kv = (rmsnorm(ckv, kv_norm_g) @ w_ukv).reshape(B, S, MLA_HEADS, QK_NOPE + V_HEAD)
    kn, v = kv[..., :QK_NOPE], kv[..., QK_NOPE:]
    o = mla_attention(qn, qr, kn, kr, v)
    return jnp.concatenate([u, o], axis=-1) @ w_out


def pool_mixer(h, pool_w, pool_scale):
    B, S, _ = h.shape
    hf = h.astype(jnp.float32).reshape(B, S, POOL_GROUPS, POOL_GROUP)
    t = jnp.arange(S)
    pooled = []
    for gi, w in enumerate(POOL_WINDOWS):
        xg = hf[:, :, gi]
        cs = jnp.cumsum(xg, axis=1)
        lag = jnp.pad(cs, ((0, 0), (w, 0), (0, 0)))[:, :S]
        cnt = jnp.minimum(t + 1, w).astype(jnp.float32)[None, :, None]
        pooled.append((cs - lag) / cnt - xg)
    p = jnp.stack(pooled, axis=2).astype(h.dtype)
    y = jnp.einsum("bsgc,gcd->bsgd", p, pool_w).reshape(B, S, D_MODEL)
    return y * pool_scale


def conv_ffn(h, w_up, ffn_conv_w, ffn_conv_b, w_down):
    uv = h @ w_up
    u, v = uv[..., :D_FF], uv[..., D_FF:]
    u = causal_dwconv(u, ffn_conv_w, ffn_conv_b)
    return (jax.nn.silu(u) * v) @ w_down


def setup_inputs(seed: int = 0) -> dict:
    key = jax.random.key(seed)
    ks = iter(jax.random.split(key, 40))
    f32 = jnp.float32

    def nrm(shape, fan_in):
        return jax.random.normal(next(ks), shape, f32) * (fan_in ** -0.5)

    def gain(shape):
        return 1.0 + 0.02 * jax.random.normal(next(ks), shape, f32)

    def bias(shape):
        return 0.02 * jax.random.normal(next(ks), shape, f32)

    x = jax.random.normal(next(ks), (BATCH, SEQ, D_MODEL), f32)
    offsets = jax.random.randint(next(ks), (BATCH, 1), 0, 1024, dtype=jnp.int32)
    positions = offsets + jnp.arange(SEQ, dtype=jnp.int32)[None, :]
    E, O, L = N_EVEN, N_ODD, DEPTH
    return {
        "x": x,
        "positions": positions,
        "norm_mix_e": gain((E, D_MODEL)),
        "w_in": nrm((E, D_MODEL, IN_COLS), D_MODEL),
        "conv_w": nrm((E, CONV_WIDTH, CONV_CH), CONV_WIDTH),
        "conv_b": bias((E, CONV_CH)),
        "conv_ln_g": gain((E, CONV_CH)),
        "conv_ln_b": bias((E, CONV_CH)),
        "q_norm_g": gain((E, Q_LORA)),
        "w_uq": nrm((E, Q_LORA, MLA_HEADS * (QK_NOPE + QK_ROPE)), Q_LORA),
        "kv_norm_g": gain((E, KV_LORA)),
        "w_ukv": nrm((E, KV_LORA, MLA_HEADS * (QK_NOPE + V_HEAD)), KV_LORA),
        "w_out": nrm((E, MIX_WIDTH, D_MODEL), MIX_WIDTH),
        "norm_mix_o": gain((O, D_MODEL)),
        "pool_w": nrm((O, POOL_GROUPS, POOL_GROUP, POOL_GROUP), POOL_GROUP),
        "pool_scale": gain((O, D_MODEL)),
        "norm_ffn": gain((L, D_MODEL)),
        "w_up": nrm((L, D_MODEL, 2 * D_FF), D_MODEL),
        "ffn_conv_w": nrm((L, FFN_CONV, D_FF), FFN_CONV),
        "ffn_conv_b": bias((L, D_FF)),
        "w_down": nrm((L, D_FF, D_MODEL), D_FF),
        "final_norm": gain((D_MODEL,)),
    }


def reference(x, positions, norm_mix_e, w_in, conv_w, conv_b, conv_ln_g, conv_ln_b,
              q_norm_g, w_uq, kv_norm_g, w_ukv, w_out, norm_mix_o, pool_w, pool_scale,
              norm_ffn, w_up, ffn_conv_w, ffn_conv_b, w_down, final_norm):
    cos, sin = rope_tables(positions)
    for l in range(DEPTH):
        if l % 2 == 0:
            e = l // 2
            h = rmsnorm(x, norm_mix_e[e])
            x = x + conv_mla_mixer(h, cos, sin, w_in[e], conv_w[e], conv_b[e],
                                   conv_ln_g[e], conv_ln_b[e], q_norm_g[e], w_uq[e],
                                   kv_norm_g[e], w_ukv[e], w_out[e])
        else:
            o = l // 2
            h = rmsnorm(x, norm_mix_o[o])
            x = x + pool_mixer(h, pool_w[o], pool_scale[o])
        h = rmsnorm(x, norm_ffn[l])
        x = x + conv_ffn(h, w_up[l], ffn_conv_w[l], ffn_conv_b[l], w_down[l])
    return rmsnorm(x, final_norm)
```

```python
import functools
import math

import jax
import jax.numpy as jnp
from jax import lax
from jax.experimental import pallas as pl
from jax.experimental.pallas import tpu as pltpu

D_MODEL = 1024
CONV_CH = 512
CONV_WIDTH = 31
HEADS = 8
QK_NOPE = 64
QK_ROPE = 32
V_HEAD = 64
Q_LORA = 384
KV_LORA = 256
ROPE_THETA = 10000.0
POOL_WINDOWS = (2, 4, 8, 16)
POOL_GROUP = 256
D_FF = 2816
FFN_CONV = 3
EPS = 1e-6

LANES = 128
HEAD_PAD = LANES
PAIR_W = 2 * HEAD_PAD
IN_COLS_PAD = 2 * CONV_CH + Q_LORA + KV_LORA + HEAD_PAD
CONV_HALO = 32
POOL_HALO = 16
FFN_HALO = 8
VMEM_LIMIT = 56 * 1024 * 1024

TS_FRONT = 512
TS_CONV = 512
TQ = 512
TM_FFN = 256

BF16 = jnp.bfloat16
F32 = jnp.float32


def _rms(x, g):
    ms = jnp.mean(x * x, axis=-1, keepdims=True)
    return x * lax.rsqrt(ms + EPS) * g


def _const_spec(shape):
    nd = len(shape)
    return pl.BlockSpec(shape, lambda *_: (0,) * nd, pipeline_mode=pl.Buffered(1))


def _front_kernel(x_ref, cos_ref, sa_ref, sb_ref, g_ref, win_ref, qg_ref, wuq_ref,
                  kvg_ref, wukv_ref, u_ref, q_ref, k_ref, v_ref):
    x = x_ref[0]
    h = _rms(x, g_ref[...]).astype(BF16)
    proj = jnp.dot(h, win_ref[...], preferred_element_type=F32)
    a = proj[:, :CONV_CH]
    g = proj[:, CONV_CH:2 * CONV_CH]
    u_ref[0] = a * jax.nn.sigmoid(g)
    c0 = 2 * CONV_CH
    cq = _rms(proj[:, c0:c0 + Q_LORA], qg_ref[...]).astype(BF16)
    ckv = _rms(proj[:, c0 + Q_LORA:c0 + Q_LORA + KV_LORA], kvg_ref[...]).astype(BF16)
    krb = proj[:, c0 + Q_LORA + KV_LORA:]
    q = jnp.dot(cq, wuq_ref[...], preferred_element_type=F32)
    kv = jnp.dot(ckv, wukv_ref[...], preferred_element_type=F32)
    cos = cos_ref[0]
    sa = sa_ref[0]
    sb = sb_ref[0]

    def rope(t):
        return (t * cos + pltpu.roll(t, HEAD_PAD - QK_ROPE // 2, 1) * sa
                + pltpu.roll(t, QK_ROPE // 2, 1) * sb)

    scale = 1.0 / math.sqrt(QK_NOPE + QK_ROPE)
    kr = rope(krb)
    for hd in range(HEADS):
        sl = slice(hd * HEAD_PAD, (hd + 1) * HEAD_PAD)
        q_ref[0, :, sl] = (rope(q[:, sl]) * scale).astype(BF16)
        k_ref[0, :, sl] = (kv[:, sl] + kr).astype(BF16)
    v_ref[0] = kv[:, HEADS * HEAD_PAD:].astype(BF16)


def _front(x, cos_t, sa_t, sb_t, g, w_in_p, qg, w_uq_p, kvg, w_ukv_p):
    B, S, _ = x.shape
    ts = TS_FRONT
    tile = lambda w: pl.BlockSpec((1, ts, w), lambda b, i: (b, i, 0))
    return pl.pallas_call(
        _front_kernel,
        grid=(B, S // ts),
        in_specs=[tile(D_MODEL), tile(HEAD_PAD), tile(HEAD_PAD), tile(HEAD_PAD),
                  _const_spec(g.shape), _const_spec(w_in_p.shape), _const_spec(qg.shape),
                  _const_spec(w_uq_p.shape), _const_spec(kvg.shape), _const_spec(w_ukv_p.shape)],
        out_specs=[tile(CONV_CH), tile(HEADS * HEAD_PAD), tile(HEADS * HEAD_PAD),
                   tile(HEADS * V_HEAD)],
        out_shape=[jax.ShapeDtypeStruct((B, S, CONV_CH), F32),
                   jax.ShapeDtypeStruct((B, S, HEADS * HEAD_PAD), BF16),
                   jax.ShapeDtypeStruct((B, S, HEADS * HEAD_PAD), BF16),
                   jax.ShapeDtypeStruct((B, S, HEADS * V_HEAD), BF16)],
        compiler_params=pltpu.CompilerParams(
            dimension_semantics=("arbitrary", "arbitrary"), vmem_limit_bytes=VMEM_LIMIT),
        name="front",
    )(x, cos_t, sa_t, sb_t, g, w_in_p, qg, w_uq_p, kvg, w_ukv_p)


CONV_ROWS = 64


def _conv_kernel(u_ref, w_ref, b_ref, lg_ref, lb_ref, o_ref, buf_ref):
    ts = u_ref.shape[1]
    i = pl.program_id(1)

    @pl.when(i == 0)
    def _():
        buf_ref[0:CONV_HALO, :] = jnp.zeros((CONV_HALO, CONV_CH), F32)

    buf_ref[CONV_HALO:CONV_HALO + ts, :] = u_ref[0]
    base = CONV_HALO - (CONV_WIDTH - 1)
    for r0 in range(0, ts, CONV_ROWS):
        acc = jnp.zeros((CONV_ROWS, CONV_CH), F32) + b_ref[...]
        for k in range(CONV_WIDTH):
            acc = acc + buf_ref[base + r0 + k:base + r0 + k + CONV_ROWS, :] * w_ref[k:k + 1, :]
        mu = jnp.mean(acc, axis=-1, keepdims=True)
        d = acc - mu
        var = jnp.mean(d * d, axis=-1, keepdims=True)
        y = d * lax.rsqrt(var + EPS) * lg_ref[...] + lb_ref[...]
        o_ref[0, r0:r0 + CONV_ROWS, :] = (y * jax.nn.sigmoid(y)).astype(BF16)
    buf_ref[0:CONV_HALO, :] = buf_ref[ts:ts + CONV_HALO, :]


def _conv_module(u, w, b, lg, lb):
    B, S, _ = u.shape
    ts = TS_CONV
    tile = pl.BlockSpec((1, ts, CONV_CH), lambda bb, i: (bb, i, 0))
    return pl.pallas_call(
        _conv_kernel,
        grid=(B, S // ts),
        in_specs=[tile, _const_spec(w.shape), _const_spec(b.shape), _const_spec(lg.shape),
                  _const_spec(lb.shape)],
        out_specs=tile,
        out_shape=jax.ShapeDtypeStruct((B, S, CONV_CH), BF16),
        scratch_shapes=[pltpu.VMEM((CONV_HALO + ts, CONV_CH), F32)],
        compiler_params=pltpu.CompilerParams(
            dimension_semantics=("arbitrary", "arbitrary"), vmem_limit_bytes=VMEM_LIMIT),
        name="conv_module",
    )(u, w, b, lg, lb)


def _attn_kernel(q_ref, k_ref, v_ref, o_ref):
    tq = q_ref.shape[1]
    tk = tq
    i = pl.program_id(2)
    neg = jnp.finfo(F32).min

    def step(j, carry, masked):
        koff = pl.multiple_of(j * tk, tk)
        vc = v_ref[0, pl.ds(koff, tk), :]
        out = []
        for hh in range(2):
            m, l, acc = carry[hh]
            qh = q_ref[0, :, hh * HEAD_PAD:(hh + 1) * HEAD_PAD]
            kc = k_ref[0, pl.ds(koff, tk), hh * HEAD_PAD:(hh + 1) * HEAD_PAD]
            s = lax.dot_general(qh, kc, (((1,), (1,)), ((), ())),
                                preferred_element_type=F32)
            if masked:
                row = lax.broadcasted_iota(jnp.int32, (tq, tk), 0)
                col = lax.broadcasted_iota(jnp.int32, (tq, tk), 1)
                s = jnp.where(row >= col, s, neg)
            m_new = jnp.maximum(m, jnp.max(s, axis=-1, keepdims=True))
            alpha = jnp.exp(m - m_new)
            p = jnp.exp(s - m_new)
            l = alpha * l + jnp.sum(p, axis=-1, keepdims=True)
            acc = alpha * acc + jnp.dot(p.astype(BF16), vc, preferred_element_type=F32)
            out.append((m_new, l, acc))
        return tuple(out)

    init = tuple((jnp.full((tq, 1), neg, F32), jnp.zeros((tq, 1), F32),
                  jnp.zeros((tq, HEAD_PAD), F32)) for _ in range(2))
    carry = lax.fori_loop(0, i, lambda j, c: step(j, c, False), init)
    carry = step(i, carry, True)
    o0 = carry[0][2] / carry[0][1]
    o1 = carry[1][2] / carry[1][1]
    lane = lax.broadcasted_iota(jnp.int32, (tq, HEAD_PAD), 1)
    o_ref[0] = jnp.where(lane < V_HEAD, o0, o1).astype(BF16)


def _attention(q, k, v):
    B, S, _ = q.shape
    tq = TQ
    npair = HEADS // 2
    return pl.pallas_call(
        _attn_kernel,
        grid=(B, npair, S // tq),
        in_specs=[pl.BlockSpec((1, tq, PAIR_W), lambda b, p, i: (b, i, p)),
                  pl.BlockSpec((1, S, PAIR_W), lambda b, p, i: (b, 0, p)),
                  pl.BlockSpec((1, S, 2 * V_HEAD), lambda b, p, i: (b, 0, p))],
        out_specs=pl.BlockSpec((1, tq, 2 * V_HEAD), lambda b, p, i: (b, i, p)),
        out_shape=jax.ShapeDtypeStruct((B, S, HEADS * V_HEAD), BF16),
        compiler_params=pltpu.CompilerParams(
            dimension_semantics=("arbitrary", "arbitrary", "arbitrary"),
            vmem_limit_bytes=VMEM_LIMIT),
        name="attention",
    )(q, k, v)


def _ffn_body(x1, i, ng_ref, wup_ref, cw_ref, cb_ref, wdn_ref, ubuf_ref):
    tm = x1.shape[0]
    h = _rms(x1, ng_ref[...]).astype(BF16)
    uv = jnp.dot(h, wup_ref[...], preferred_element_type=F32)

    @pl.when(i == 0)
    def _():
        ubuf_ref[0:FFN_HALO, :] = jnp.zeros((FFN_HALO, D_FF), F32)

    ubuf_ref[FFN_HALO:FFN_HALO + tm, :] = uv[:, :D_FF]
    c = cb_ref[...] + ubuf_ref[FFN_HALO:FFN_HALO + tm, :] * cw_ref[2:3, :]
    c = c + ubuf_ref[FFN_HALO - 1:FFN_HALO - 1 + tm, :] * cw_ref[1:2, :]
    c = c + ubuf_ref[FFN_HALO - 2:FFN_HALO - 2 + tm, :] * cw_ref[0:1, :]
    act = (c * jax.nn.sigmoid(c) * uv[:, D_FF:]).astype(BF16)
    ubuf_ref[0:FFN_HALO, :] = ubuf_ref[tm:tm + FFN_HALO, :]
    return x1 + jnp.dot(act, wdn_ref[...], preferred_element_type=F32)


def _ffn0_kernel(x_ref, u_ref, o_ref, wout_ref, ng_ref, wup_ref, cw_ref, cb_ref, wdn_ref,
                 out_ref, ubuf_ref):
    i = pl.program_id(1)
    mix = jnp.concatenate([u_ref[0], o_ref[0]], axis=1)
    x1 = x_ref[0] + jnp.dot(mix, wout_ref[...], preferred_element_type=F32)
    out_ref[0] = _ffn_body(x1, i, ng_ref, wup_ref, cw_ref, cb_ref, wdn_ref, ubuf_ref)


def _ffn1_kernel(x_ref, mg_ref, pw_ref, ps_ref, ng_ref, wup_ref, cw_ref, cb_ref, wdn_ref,
                 fg_ref, out_ref, ubuf_ref, hbuf_ref):
    tm = x_ref.shape[1]
    i = pl.program_id(1)
    x = x_ref[0]

    @pl.when(i == 0)
    def _():
        hbuf_ref[0:POOL_HALO, :] = jnp.zeros((POOL_HALO, D_MODEL), F32)

    hbuf_ref[POOL_HALO:POOL_HALO + tm, :] = _rms(x, mg_ref[...])
    t = i * tm + lax.broadcasted_iota(jnp.int32, (tm, 1), 0)
    ys = []
    for gi, w in enumerate(POOL_WINDOWS):
        cs = slice(gi * POOL_GROUP, (gi + 1) * POOL_GROUP)
        hg = hbuf_ref[POOL_HALO:POOL_HALO + tm, cs]
        tot = hg
        for d in range(1, w):
            tot = tot + hbuf_ref[POOL_HALO - d:POOL_HALO - d + tm, cs]
        cnt = jnp.minimum(t + 1, w).astype(F32)
        pooled = (tot / cnt - hg).astype(BF16)
        ys.append(jnp.dot(pooled, pw_ref[gi], preferred_element_type=F32))
    hbuf_ref[0:POOL_HALO, :] = hbuf_ref[tm:tm + POOL_HALO, :]
    x1 = x + jnp.concatenate(ys, axis=1) * ps_ref[...]
    x2 = _ffn_body(x1, i, ng_ref, wup_ref, cw_ref, cb_ref, wdn_ref, ubuf_ref)
    out_ref[0] = _rms(x2, fg_ref[...])


def _ffn_call(kernel, name, acts, consts, extra_scratch, B, S):
    tm = TM_FFN
    act_specs = [pl.BlockSpec((1, tm, a.shape[-1]), lambda b, i: (b, i, 0)) for a in acts]
    return pl.pallas_call(
        kernel,
        grid=(B, S // tm),
        in_specs=act_specs + [_const_spec(c.shape) for c in consts],
        out_specs=pl.BlockSpec((1, tm, D_MODEL), lambda b, i: (b, i, 0)),
        out_shape=jax.ShapeDtypeStruct((B, S, D_MODEL), F32),
        scratch_shapes=[pltpu.VMEM((FFN_HALO + tm, D_FF), F32)] + extra_scratch,
        compiler_params=pltpu.CompilerParams(
            dimension_semantics=("arbitrary", "arbitrary"), vmem_limit_bytes=VMEM_LIMIT),
        name=name,
    )(*acts, *consts)


def _rope_tables(positions):
    inv_freq = 1.0 / (ROPE_THETA ** (jnp.arange(0, QK_ROPE, 2, dtype=F32) / QK_ROPE))
    ang = positions.astype(F32)[..., None] * inv_freq
    cos, sin = jnp.cos(ang), jnp.sin(ang)
    half = QK_ROPE // 2
    shp = cos.shape[:-1]
    ones = jnp.ones(shp + (QK_NOPE,), F32)
    zeros = lambda n: jnp.zeros(shp + (n,), F32)
    tail = HEAD_PAD - QK_NOPE - QK_ROPE
    cos_t = jnp.concatenate([ones, cos, cos, zeros(tail)], axis=-1)
    sa_t = jnp.concatenate([zeros(QK_NOPE), -sin, zeros(half + tail)], axis=-1)
    sb_t = jnp.concatenate([zeros(QK_NOPE + half), sin, zeros(tail)], axis=-1)
    return cos_t, sa_t, sb_t


def _pad_heads(w, width, take):
    K = w.shape[0]
    w = w.reshape(K, HEADS, width)[:, :, :take]
    w = jnp.pad(w, ((0, 0), (0, 0), (0, HEAD_PAD - take)))
    return w.reshape(K, HEADS * HEAD_PAD)


def kernel(x, positions, norm_mix_e, w_in, conv_w, conv_b, conv_ln_g, conv_ln_b, q_norm_g, w_uq, kv_norm_g, w_ukv, w_out, norm_mix_o, pool_w, pool_scale, norm_ffn, w_up, ffn_conv_w, ffn_conv_b, w_down, final_norm):
    B, S, _ = x.shape
    row = lambda v: v.reshape(1, -1)
    cos_t, sa_t, sb_t = _rope_tables(positions)

    c0 = 2 * CONV_CH + Q_LORA + KV_LORA
    kr_cols = jnp.pad(w_in[0][:, c0:], ((0, 0), (QK_NOPE, HEAD_PAD - QK_NOPE - QK_ROPE)))
    w_in_p = jnp.concatenate([w_in[0][:, :c0], kr_cols], axis=1).astype(BF16)
    w_uq_p = _pad_heads(w_uq[0], QK_NOPE + QK_ROPE, QK_NOPE + QK_ROPE).astype(BF16)
    w_k = _pad_heads(w_ukv[0], QK_NOPE + V_HEAD, QK_NOPE)
    w_v = w_ukv[0].reshape(KV_LORA, HEADS, QK_NOPE + V_HEAD)[:, :, QK_NOPE:].reshape(KV_LORA, -1)
    w_ukv_p = jnp.concatenate([w_k, w_v], axis=1).astype(BF16)

    u, q, k, v = _front(x, cos_t, sa_t, sb_t, row(norm_mix_e[0]), w_in_p, row(q_norm_g[0]),
                        w_uq_p, row(kv_norm_g[0]), w_ukv_p)
    u_act = _conv_module(u, conv_w[0], row(conv_b[0]), row(conv_ln_g[0]), row(conv_ln_b[0]))
    o = _attention(q, k, v)

    ffn_consts = lambda l: [row(norm_ffn[l]), w_up[l].astype(BF16), ffn_conv_w[l],
                            row(ffn_conv_b[l]), w_down[l].astype(BF16)]
    x = _ffn_call(_ffn0_kernel, "ffn0", [x, u_act, o],
                  [w_out[0].astype(BF16)] + ffn_consts(0), [], B, S)
    x = _ffn_call(_ffn1_kernel, "ffn1", [x],
                  [row(norm_mix_o[0]), pool_w[0].astype(BF16), row(pool_scale[0])]
                  + ffn_consts(1) + [row(final_norm)],
                  [pltpu.VMEM((POOL_HALO + TM_FFN, D_MODEL), F32)], B, S)
    return x
```

```python
import functools
import math

import jax
import jax.numpy as jnp
from jax import lax
from jax.experimental import pallas as pl
from jax.experimental.pallas import tpu as pltpu

D_MODEL = 1024
CONV_CH = 512
CONV_WIDTH = 31
HEADS = 8
QK_NOPE = 64
QK_ROPE = 32
V_HEAD = 64
Q_LORA = 384
KV_LORA = 256
ROPE_THETA = 10000.0
POOL_WINDOWS = (2, 4, 8, 16)
POOL_GROUP = 256
D_FF = 2816
FFN_CONV = 3
EPS = 1e-6

LANES = 128
HEAD_PAD = LANES
PAIR_W = 2 * HEAD_PAD
IN_COLS_PAD = 2 * CONV_CH + Q_LORA + KV_LORA + HEAD_PAD
CONV_HALO = 32
POOL_HALO = 16
FFN_HALO = 8
VMEM_LIMIT = 56 * 1024 * 1024

TS_FRONT = 512
TS_CONV = 512
TQ = 512
TK = 256
TM_FFN = 256

BF16 = jnp.bfloat16
F32 = jnp.float32


def _rms(x, g):
    ms = jnp.mean(x * x, axis=-1, keepdims=True)
    return x * lax.rsqrt(ms + EPS) * g


def _const_spec(shape):
    nd = len(shape)
    return pl.BlockSpec(shape, lambda *_: (0,) * nd, pipeline_mode=pl.Buffered(1))


def _front_kernel(x_ref, cos_ref, sa_ref, sb_ref, g_ref, win_ref, qg_ref, wuq_ref,
                  kvg_ref, wukv_ref, u_ref, q_ref, k_ref, vt_ref):
    x = x_ref[0]
    h = _rms(x, g_ref[...]).astype(BF16)
    proj = jnp.dot(h, win_ref[...], preferred_element_type=F32)
    a = proj[:, :CONV_CH]
    g = proj[:, CONV_CH:2 * CONV_CH]
    u_ref[0] = a * jax.nn.sigmoid(g)
    c0 = 2 * CONV_CH
    cq = _rms(proj[:, c0:c0 + Q_LORA], qg_ref[...]).astype(BF16)
    ckv = _rms(proj[:, c0 + Q_LORA:c0 + Q_LORA + KV_LORA], kvg_ref[...]).astype(BF16)
    krb = proj[:, c0 + Q_LORA + KV_LORA:]
    q = jnp.dot(cq, wuq_ref[...], preferred_element_type=F32)
    kv = jnp.dot(ckv, wukv_ref[...], preferred_element_type=F32)
    cos = cos_ref[0]
    sa = sa_ref[0]
    sb = sb_ref[0]

    def rope(t):
        return (t * cos + pltpu.roll(t, HEAD_PAD - QK_ROPE // 2, 1) * sa
                + pltpu.roll(t, QK_ROPE // 2, 1) * sb)

    scale = math.log2(math.e) / math.sqrt(QK_NOPE + QK_ROPE)
    kr = rope(krb)
    for hd in range(HEADS):
        sl = slice(hd * HEAD_PAD, (hd + 1) * HEAD_PAD)
        q_ref[0, :, sl] = (rope(q[:, sl]) * scale).astype(BF16)
        k_ref[0, :, sl] = (kv[:, sl] + kr).astype(BF16)
    v = kv[:, HEADS * HEAD_PAD:]
    for c in range(v.shape[0] // TK):
        vt_ref[0, c] = v[c * TK:(c + 1) * TK, :].T.astype(BF16)


def _front(x, cos_t, sa_t, sb_t, g, w_in_p, qg, w_uq_p, kvg, w_ukv_p):
    B, S, _ = x.shape
    ts = TS_FRONT
    tile = lambda w: pl.BlockSpec((1, ts, w), lambda b, i: (b, i, 0))
    return pl.pallas_call(
        _front_kernel,
        grid=(B, S // ts),
        in_specs=[tile(D_MODEL), tile(HEAD_PAD), tile(HEAD_PAD), tile(HEAD_PAD),
                  _const_spec(g.shape), _const_spec(w_in_p.shape), _const_spec(qg.shape),
                  _const_spec(w_uq_p.shape), _const_spec(kvg.shape), _const_spec(w_ukv_p.shape)],
        out_specs=[tile(CONV_CH), tile(HEADS * HEAD_PAD), tile(HEADS * HEAD_PAD),
                   pl.BlockSpec((1, ts // TK, HEADS * V_HEAD, TK), lambda b, i: (b, i, 0, 0))],
        out_shape=[jax.ShapeDtypeStruct((B, S, CONV_CH), F32),
                   jax.ShapeDtypeStruct((B, S, HEADS * HEAD_PAD), BF16),
                   jax.ShapeDtypeStruct((B, S, HEADS * HEAD_PAD), BF16),
                   jax.ShapeDtypeStruct((B, S // TK, HEADS * V_HEAD, TK), BF16)],
        compiler_params=pltpu.CompilerParams(
            dimension_semantics=("arbitrary", "arbitrary"), vmem_limit_bytes=VMEM_LIMIT),
        name="front",
    )(x, cos_t, sa_t, sb_t, g, w_in_p, qg, w_uq_p, kvg, w_ukv_p)


CONV_ROWS = 64


def _conv_kernel(u_ref, w_ref, b_ref, lg_ref, lb_ref, o_ref, buf_ref):
    ts = u_ref.shape[1]
    i = pl.program_id(1)

    @pl.when(i == 0)
    def _():
        buf_ref[0:CONV_HALO, :] = jnp.zeros((CONV_HALO, CONV_CH), F32)

    buf_ref[CONV_HALO:CONV_HALO + ts, :] = u_ref[0]
    base = CONV_HALO - (CONV_WIDTH - 1)
    for r0 in range(0, ts, CONV_ROWS):
        acc = jnp.zeros((CONV_ROWS, CONV_CH), F32) + b_ref[...]
        for k in range(CONV_WIDTH):
            acc = acc + buf_ref[base + r0 + k:base + r0 + k + CONV_ROWS, :] * w_ref[k:k + 1, :]
        mu = jnp.mean(acc, axis=-1, keepdims=True)
        d = acc - mu
        var = jnp.mean(d * d, axis=-1, keepdims=True)
        y = d * lax.rsqrt(var + EPS) * lg_ref[...] + lb_ref[...]
        o_ref[0, r0:r0 + CONV_ROWS, :] = (y * jax.nn.sigmoid(y)).astype(BF16)
    buf_ref[0:CONV_HALO, :] = buf_ref[ts:ts + CONV_HALO, :]


def _conv_module(u, w, b, lg, lb):
    B, S, _ = u.shape
    ts = TS_CONV
    tile = pl.BlockSpec((1, ts, CONV_CH), lambda bb, i: (bb, i, 0))
    return pl.pallas_call(
        _conv_kernel,
        grid=(B, S // ts),
        in_specs=[tile, _const_spec(w.shape), _const_spec(b.shape), _const_spec(lg.shape),
                  _const_spec(lb.shape)],
        out_specs=tile,
        out_shape=jax.ShapeDtypeStruct((B, S, CONV_CH), BF16),
        scratch_shapes=[pltpu.VMEM((CONV_HALO + ts, CONV_CH), F32)],
        compiler_params=pltpu.CompilerParams(
            dimension_semantics=("arbitrary", "arbitrary"), vmem_limit_bytes=VMEM_LIMIT),
        name="conv_module",
    )(u, w, b, lg, lb)


def _attn_kernel(q_ref, k_ref, vt_ref, o_ref, m_ref, l_ref, acc_ref, s_ref, cm_ref):
    tq = q_ref.shape[1]
    i = pl.program_id(2)
    neg = jnp.finfo(F32).min
    m_ref[...] = jnp.full(m_ref.shape, neg, F32)
    l_ref[...] = jnp.zeros(l_ref.shape, F32)
    acc_ref[...] = jnp.zeros(acc_ref.shape, F32)
    ndiag = tq // TK
    nfull = i * ndiag

    def scores(j, slot, diag):
        koff = pl.multiple_of(j * TK, TK)
        for hh in range(2):
            qh = q_ref[0, :, hh * HEAD_PAD:(hh + 1) * HEAD_PAD]
            kc = k_ref[0, pl.ds(koff, TK), hh * HEAD_PAD:(hh + 1) * HEAD_PAD]
            st = lax.dot_general(kc, qh, (((1,), (1,)), ((), ())),
                                 preferred_element_type=F32)
            if diag is not None:
                key = lax.broadcasted_iota(jnp.int32, (TK, tq), 0) + diag * TK
                qry = lax.broadcasted_iota(jnp.int32, (TK, tq), 1)
                st = jnp.where(key <= qry, st, neg)
            s_ref[slot, hh] = st
            cm_ref[slot, hh] = jnp.max(st, axis=0, keepdims=True)

    def update(j, slot):
        vt = vt_ref[0, j]
        for hh in range(2):
            m_old = m_ref[hh]
            m_new = jnp.maximum(m_old, cm_ref[slot, hh])
            alpha = jnp.exp2(m_old - m_new)
            p = jnp.exp2(s_ref[slot, hh] - m_new)
            l_ref[hh] = alpha * l_ref[hh] + jnp.sum(p, axis=0, keepdims=True)
            pv = jnp.dot(vt, p.astype(BF16), preferred_element_type=F32)
            acc_ref[hh] = alpha * acc_ref[hh] + pv
            m_ref[hh] = m_new

    assert ndiag == 2
    scores(nfull, 0, 0)
    scores(nfull + 1, 1, 1)
    update(nfull, 0)

    def two_chunks(t, c):
        scores(2 * t, 0, None)
        update(jnp.where(t == 0, nfull + 1, 2 * t - 1), 1)
        scores(2 * t + 1, 1, None)
        update(2 * t, 0)
        return c

    lax.fori_loop(0, i, two_chunks, 0)
    update(nfull + 1 - 2 * jnp.minimum(i, 1), 1)
    o0 = acc_ref[0, :V_HEAD, :] / l_ref[0]
    o1 = acc_ref[1, V_HEAD:, :] / l_ref[1]
    o_ref[0] = jnp.concatenate([o0, o1], axis=0).T.astype(BF16)


def _attention(q, k, vt):
    B, S, _ = q.shape
    tq = TQ
    npair = HEADS // 2
    return pl.pallas_call(
        _attn_kernel,
        grid=(B, npair, S // tq),
        in_specs=[pl.BlockSpec((1, tq, PAIR_W), lambda b, p, i: (b, i, p)),
                  pl.BlockSpec((1, S, PAIR_W), lambda b, p, i: (b, 0, p)),
                  pl.BlockSpec((1, S // TK, 2 * V_HEAD, TK), lambda b, p, i: (b, 0, p, 0))],
        out_specs=pl.BlockSpec((1, tq, 2 * V_HEAD), lambda b, p, i: (b, i, p)),
        out_shape=jax.ShapeDtypeStruct((B, S, HEADS * V_HEAD), BF16),
        scratch_shapes=[pltpu.VMEM((2, 1, tq), F32), pltpu.VMEM((2, 1, tq), F32),
                        pltpu.VMEM((2, 2 * V_HEAD, tq), F32),
                        pltpu.VMEM((2, 2, TK, tq), F32), pltpu.VMEM((2, 2, 1, tq), F32)],
        compiler_params=pltpu.CompilerParams(
            dimension_semantics=("arbitrary", "arbitrary", "arbitrary"),
            vmem_limit_bytes=VMEM_LIMIT),
        name="attention",
    )(q, k, vt)


def _ffn_body(x1, i, ng_ref, wup_ref, cw_ref, cb_ref, wdn_ref, ubuf_ref):
    tm = x1.shape[0]
    h = _rms(x1, ng_ref[...]).astype(BF16)
    uv = jnp.dot(h, wup_ref[...], preferred_element_type=F32)

    @pl.when(i == 0)
    def _():
        ubuf_ref[0:FFN_HALO, :] = jnp.zeros((FFN_HALO, D_FF), F32)

    ubuf_ref[FFN_HALO:FFN_HALO + tm, :] = uv[:, :D_FF]
    c = cb_ref[...] + ubuf_ref[FFN_HALO:FFN_HALO + tm, :] * cw_ref[2:3, :]
    c = c + ubuf_ref[FFN_HALO - 1:FFN_HALO - 1 + tm, :] * cw_ref[1:2, :]
    c = c + ubuf_ref[FFN_HALO - 2:FFN_HALO - 2 + tm, :] * cw_ref[0:1, :]
    act = (c * jax.nn.sigmoid(c) * uv[:, D_FF:]).astype(BF16)
    ubuf_ref[0:FFN_HALO, :] = ubuf_ref[tm:tm + FFN_HALO, :]
    return x1 + jnp.dot(act, wdn_ref[...], preferred_element_type=F32)


def _ffn0_kernel(x_ref, u_ref, o_ref, wout_ref, ng_ref, wup_ref, cw_ref, cb_ref, wdn_ref,
                 out_ref, ubuf_ref):
    i = pl.program_id(1)
    mix = jnp.concatenate([u_ref[0], o_ref[0]], axis=1)
    x1 = x_ref[0] + jnp.dot(mix, wout_ref[...], preferred_element_type=F32)
    out_ref[0] = _ffn_body(x1, i, ng_ref, wup_ref, cw_ref, cb_ref, wdn_ref, ubuf_ref)


def _ffn1_kernel(x_ref, mg_ref, pw_ref, ps_ref, ng_ref, wup_ref, cw_ref, cb_ref, wdn_ref,
                 fg_ref, out_ref, ubuf_ref, hbuf_ref):
    tm = x_ref.shape[1]
    i = pl.program_id(1)
    x = x_ref[0]

    @pl.when(i == 0)
    def _():
        hbuf_ref[0:POOL_HALO, :] = jnp.zeros((POOL_HALO, D_MODEL), F32)

    hbuf_ref[POOL_HALO:POOL_HALO + tm, :] = _rms(x, mg_ref[...])
    t = i * tm + lax.broadcasted_iota(jnp.int32, (tm, 1), 0)
    ys = []
    for gi, w in enumerate(POOL_WINDOWS):
        cs = slice(gi * POOL_GROUP, (gi + 1) * POOL_GROUP)
        hg = hbuf_ref[POOL_HALO:POOL_HALO + tm, cs]
        tot = hg
        for d in range(1, w):
            tot = tot + hbuf_ref[POOL_HALO - d:POOL_HALO - d + tm, cs]
        cnt = jnp.minimum(t + 1, w).astype(F32)
        pooled = (tot / cnt - hg).astype(BF16)
        ys.append(jnp.dot(pooled, pw_ref[gi], preferred_element_type=F32))
    hbuf_ref[0:POOL_HALO, :] = hbuf_ref[tm:tm + POOL_HALO, :]
    x1 = x + jnp.concatenate(ys, axis=1) * ps_ref[...]
    x2 = _ffn_body(x1, i, ng_ref, wup_ref, cw_ref, cb_ref, wdn_ref, ubuf_ref)
    out_ref[0] = _rms(x2, fg_ref[...])


def _ffn_call(kernel, name, acts, consts, extra_scratch, B, S):
    tm = TM_FFN
    act_specs = [pl.BlockSpec((1, tm, a.shape[-1]), lambda b, i: (b, i, 0)) for a in acts]
    return pl.pallas_call(
        kernel,
        grid=(B, S // tm),
        in_specs=act_specs + [_const_spec(c.shape) for c in consts],
        out_specs=pl.BlockSpec((1, tm, D_MODEL), lambda b, i: (b, i, 0)),
        out_shape=jax.ShapeDtypeStruct((B, S, D_MODEL), F32),
        scratch_shapes=[pltpu.VMEM((FFN_HALO + tm, D_FF), F32)] + extra_scratch,
        compiler_params=pltpu.CompilerParams(
            dimension_semantics=("arbitrary", "arbitrary"), vmem_limit_bytes=VMEM_LIMIT),
        name=name,
    )(*acts, *consts)


def _rope_tables(positions):
    inv_freq = 1.0 / (ROPE_THETA ** (jnp.arange(0, QK_ROPE, 2, dtype=F32) / QK_ROPE))
    ang = positions.astype(F32)[..., None] * inv_freq
    cos, sin = jnp.cos(ang), jnp.sin(ang)
    half = QK_ROPE // 2
    shp = cos.shape[:-1]
    ones = jnp.ones(shp + (QK_NOPE,), F32)
    zeros = lambda n: jnp.zeros(shp + (n,), F32)
    tail = HEAD_PAD - QK_NOPE - QK_ROPE
    cos_t = jnp.concatenate([ones, cos, cos, zeros(tail)], axis=-1)
    sa_t = jnp.concatenate([zeros(QK_NOPE), -sin, zeros(half + tail)], axis=-1)
    sb_t = jnp.concatenate([zeros(QK_NOPE + half), sin, zeros(tail)], axis=-1)
    return cos_t, sa_t, sb_t


def _pad_heads(w, width, take):
    K = w.shape[0]
    w = w.reshape(K, HEADS, width)[:, :, :take]
    w = jnp.pad(w, ((0, 0), (0, 0), (0, HEAD_PAD - take)))
    return w.reshape(K, HEADS * HEAD_PAD)


def kernel(x, positions, norm_mix_e, w_in, conv_w, conv_b, conv_ln_g, conv_ln_b, q_norm_g, w_uq, kv_norm_g, w_ukv, w_out, norm_mix_o, pool_w, pool_scale, norm_ffn, w_up, ffn_conv_w, ffn_conv_b, w_down, final_norm):
    B, S, _ = x.shape
    row = lambda v: v.reshape(1, -1)
    cos_t, sa_t, sb_t = _rope_tables(positions)

    c0 = 2 * CONV_CH + Q_LORA + KV_LORA
    kr_cols = jnp.pad(w_in[0][:, c0:], ((0, 0), (QK_NOPE, HEAD_PAD - QK_NOPE - QK_ROPE)))
    w_in_p = jnp.concatenate([w_in[0][:, :c0], kr_cols], axis=1).astype(BF16)
    w_uq_p = _pad_heads(w_uq[0], QK_NOPE + QK_ROPE, QK_NOPE + QK_ROPE).astype(BF16)
    w_k = _pad_heads(w_ukv[0], QK_NOPE + V_HEAD, QK_NOPE)
    w_v = w_ukv[0].reshape(KV_LORA, HEADS, QK_NOPE + V_HEAD)[:, :, QK_NOPE:].reshape(KV_LORA, -1)
    w_ukv_p = jnp.concatenate([w_k, w_v], axis=1).astype(BF16)

    u, q, k, vt = _front(x, cos_t, sa_t, sb_t, row(norm_mix_e[0]), w_in_p, row(q_norm_g[0]),
                        w_uq_p, row(kv_norm_g[0]), w_ukv_p)
    u_act = _conv_module(u, conv_w[0], row(conv_b[0]), row(conv_ln_g[0]), row(conv_ln_b[0]))
    o = _attention(q, k, vt)

    ffn_consts = lambda l: [row(norm_ffn[l]), w_up[l].astype(BF16), ffn_conv_w[l],
                            row(ffn_conv_b[l]), w_down[l].astype(BF16)]
    x = _ffn_call(_ffn0_kernel, "ffn0", [x, u_act, o],
                  [w_out[0].astype(BF16)] + ffn_consts(0), [], B, S)
    x = _ffn_call(_ffn1_kernel, "ffn1", [x],
                  [row(norm_mix_o[0]), pool_w[0].astype(BF16), row(pool_scale[0])]
                  + ffn_consts(1) + [row(final_norm)],
                  [pltpu.VMEM((POOL_HALO + TM_FFN, D_MODEL), F32)], B, S)
    return x
```

```python
import functools
import math

import jax
import jax.numpy as jnp
from jax import lax
from jax.experimental import pallas as pl
from jax.experimental.pallas import tpu as pltpu

D_MODEL = 1024
CONV_CH = 512
CONV_WIDTH = 31
HEADS = 8
QK_NOPE = 64
QK_ROPE = 32
V_HEAD = 64
Q_LORA = 384
KV_LORA = 256
ROPE_THETA = 10000.0
POOL_WINDOWS = (2, 4, 8, 16)
POOL_GROUP = 256
D_FF = 2816
FFN_CONV = 3
EPS = 1e-6

LANES = 128
HEAD_PAD = LANES
PAIR_W = 2 * HEAD_PAD
IN_COLS_PAD = 2 * CONV_CH + Q_LORA + KV_LORA + HEAD_PAD
CONV_HALO = 32
POOL_HALO = 16
FFN_CHUNK = 256
VMEM_LIMIT = 56 * 1024 * 1024

TS_FRONT = 1024
TS_CONV = 512
TQ = 512
TK = 256
TM_FFN = 256

BF16 = jnp.bfloat16
F32 = jnp.float32


def _rms(x, g):
    ms = jnp.mean(x * x, axis=-1, keepdims=True)
    return x * lax.rsqrt(ms + EPS) * g


def _const_spec(shape):
    nd = len(shape)
    return pl.BlockSpec(shape, lambda *_: (0,) * nd, pipeline_mode=pl.Buffered(1))


def _front_kernel(x_ref, cos_ref, sa_ref, sb_ref, g_ref, win_ref, qg_ref, wuq_ref,
                  kvg_ref, wukv_ref, u_ref, q_ref, k_ref, vt_ref):
    nblk = x_ref.shape[1] // TK
    c0 = 2 * CONV_CH
    scale = math.log2(math.e) / math.sqrt(QK_NOPE + QK_ROPE)
    ones = jnp.ones((V_HEAD, TK), BF16)

    def in_proj(r):
        h = _rms(x_ref[0, r * TK:(r + 1) * TK, :], g_ref[...]).astype(BF16)
        return jnp.dot(h, win_ref[...], preferred_element_type=F32)

    def latent_proj(r, proj):
        rows = slice(r * TK, (r + 1) * TK)
        u_ref[0, rows, :] = proj[:, :CONV_CH] * jax.nn.sigmoid(proj[:, CONV_CH:c0])
        cq = _rms(proj[:, c0:c0 + Q_LORA], qg_ref[...]).astype(BF16)
        ckv = _rms(proj[:, c0 + Q_LORA:c0 + Q_LORA + KV_LORA], kvg_ref[...]).astype(BF16)
        q = jnp.dot(cq, wuq_ref[...], preferred_element_type=F32)
        kv = jnp.dot(ckv, wukv_ref[...], preferred_element_type=F32)
        return q, kv, proj[:, c0 + Q_LORA + KV_LORA:]

    def finish(r, q, kv, krb):
        rows = slice(r * TK, (r + 1) * TK)
        cos, sa, sb = cos_ref[0, rows, :], sa_ref[0, rows, :], sb_ref[0, rows, :]

        def rope(t):
            return (t * cos + pltpu.roll(t, HEAD_PAD - QK_ROPE // 2, 1) * sa
                    + pltpu.roll(t, QK_ROPE // 2, 1) * sb)

        kr = rope(krb)
        for hd in range(HEADS):
            sl = slice(hd * HEAD_PAD, (hd + 1) * HEAD_PAD)
            q_ref[0, rows, sl] = (rope(q[:, sl]) * scale).astype(BF16)
            k_ref[0, rows, sl] = (kv[:, sl] + kr).astype(BF16)
        vt = kv[:, HEADS * HEAD_PAD:].T.astype(BF16)
        for hd in range(HEADS):
            vt_ref[0, r, hd * HEAD_PAD:hd * HEAD_PAD + V_HEAD, :] = vt[hd * V_HEAD:(hd + 1) * V_HEAD]
            vt_ref[0, r, hd * HEAD_PAD + V_HEAD:(hd + 1) * HEAD_PAD, :] = ones

    proj = in_proj(0)
    lat = None
    for r in range(nblk):
        nxt = in_proj(r + 1) if r + 1 < nblk else None
        if lat is not None:
            finish(r - 1, *lat)
        lat = latent_proj(r, proj)
        proj = nxt
    finish(nblk - 1, *lat)


def _front(x, cos_t, sa_t, sb_t, g, w_in_p, qg, w_uq_p, kvg, w_ukv_p):
    B, S, _ = x.shape
    ts = TS_FRONT
    tile = lambda w: pl.BlockSpec((1, ts, w), lambda b, i: (b, i, 0))
    return pl.pallas_call(
        _front_kernel,
        grid=(B, S // ts),
        in_specs=[tile(D_MODEL), tile(HEAD_PAD), tile(HEAD_PAD), tile(HEAD_PAD),
                  _const_spec(g.shape), _const_spec(w_in_p.shape), _const_spec(qg.shape),
                  _const_spec(w_uq_p.shape), _const_spec(kvg.shape), _const_spec(w_ukv_p.shape)],
        out_specs=[tile(CONV_CH), tile(HEADS * HEAD_PAD), tile(HEADS * HEAD_PAD),
                   pl.BlockSpec((1, ts // TK, HEADS * HEAD_PAD, TK), lambda b, i: (b, i, 0, 0))],
        out_shape=[jax.ShapeDtypeStruct((B, S, CONV_CH), F32),
                   jax.ShapeDtypeStruct((B, S, HEADS * HEAD_PAD), BF16),
                   jax.ShapeDtypeStruct((B, S, HEADS * HEAD_PAD), BF16),
                   jax.ShapeDtypeStruct((B, S // TK, HEADS * HEAD_PAD, TK), BF16)],
        compiler_params=pltpu.CompilerParams(
            dimension_semantics=("arbitrary", "arbitrary"), vmem_limit_bytes=VMEM_LIMIT),
        name="front",
    )(x, cos_t, sa_t, sb_t, g, w_in_p, qg, w_uq_p, kvg, w_ukv_p)


CONV_ROWS = 64


def _conv_kernel(u_ref, w_ref, b_ref, lg_ref, lb_ref, o_ref, buf_ref):
    ts = u_ref.shape[1]
    i = pl.program_id(1)

    @pl.when(i == 0)
    def _():
        buf_ref[0:CONV_HALO, :] = jnp.zeros((CONV_HALO, CONV_CH), F32)

    buf_ref[CONV_HALO:CONV_HALO + ts, :] = u_ref[0]
    base = CONV_HALO - (CONV_WIDTH - 1)
    rb = CONV_ROWS
    for r0 in range(0, ts, rb):
        acc = None
        for r in range(8):
            rows = rb if r == 0 else rb + 8
            z = None
            for a in range((CONV_WIDTH + base) // 8 + 1):
                k = 8 * a + r - base
                if 0 <= k < CONV_WIDTH:
                    term = buf_ref[r0 + 8 * a:r0 + 8 * a + rows, :] * w_ref[k:k + 1, :]
                    z = term if z is None else z + term
            if r == 0:
                acc = z + b_ref[...]
            else:
                acc = acc + pltpu.roll(z, rows - r, 0)[:rb]
        mu = jnp.mean(acc, axis=-1, keepdims=True)
        d = acc - mu
        var = jnp.mean(d * d, axis=-1, keepdims=True)
        hy = d * (lax.rsqrt(var + EPS) * 0.5) * lg_ref[...] + 0.5 * lb_ref[...]
        o_ref[0, r0:r0 + rb, :] = (hy + hy * jnp.tanh(hy)).astype(BF16)
    buf_ref[0:CONV_HALO, :] = buf_ref[ts:ts + CONV_HALO, :]


def _conv_module(u, w, b, lg, lb):
    B, S, _ = u.shape
    ts = TS_CONV
    tile = pl.BlockSpec((1, ts, CONV_CH), lambda bb, i: (bb, i, 0))
    return pl.pallas_call(
        _conv_kernel,
        grid=(B, S // ts),
        in_specs=[tile, _const_spec(w.shape), _const_spec(b.shape), _const_spec(lg.shape),
                  _const_spec(lb.shape)],
        out_specs=tile,
        out_shape=jax.ShapeDtypeStruct((B, S, CONV_CH), BF16),
        scratch_shapes=[pltpu.VMEM((CONV_HALO + ts, CONV_CH), F32)],
        compiler_params=pltpu.CompilerParams(
            dimension_semantics=("arbitrary", "arbitrary"), vmem_limit_bytes=VMEM_LIMIT),
        name="conv_module",
    )(u, w, b, lg, lb)


def _attn_kernel(q_ref, k_ref, vt_ref, o_ref, m_ref, acc_ref, s_ref, cm_ref):
    tq = q_ref.shape[1]
    i = pl.program_id(2)
    neg = jnp.finfo(F32).min
    m_ref[...] = jnp.full(m_ref.shape, neg, F32)
    acc_ref[...] = jnp.zeros(acc_ref.shape, F32)
    ndiag = tq // TK
    nfull = i * ndiag

    def scores(j, slot, diag):
        koff = pl.multiple_of(j * TK, TK)
        for hh in range(2):
            qh = q_ref[0, :, hh * HEAD_PAD:(hh + 1) * HEAD_PAD]
            kc = k_ref[0, pl.ds(koff, TK), hh * HEAD_PAD:(hh + 1) * HEAD_PAD]
            st = lax.dot_general(kc, qh, (((1,), (1,)), ((), ())),
                                 preferred_element_type=F32)
            if diag is not None:
                key = lax.broadcasted_iota(jnp.int32, (TK, tq), 0) + diag * TK
                qry = lax.broadcasted_iota(jnp.int32, (TK, tq), 1)
                st = jnp.where(key <= qry, st, neg)
            s_ref[slot, hh] = st
            cm_ref[slot, hh] = jnp.max(st, axis=0, keepdims=True)

    def update(j, slot):
        for hh in range(2):
            vt = vt_ref[0, j, hh * HEAD_PAD:(hh + 1) * HEAD_PAD, :]
            m_old = m_ref[hh]
            m_new = jnp.maximum(m_old, cm_ref[slot, hh])
            alpha = jnp.exp2(m_old - m_new)
            p = jnp.exp2(s_ref[slot, hh] - m_new)
            pv = jnp.dot(vt, p.astype(BF16), preferred_element_type=F32)
            acc_ref[hh] = alpha * acc_ref[hh] + pv
            m_ref[hh] = m_new

    assert ndiag == 2
    scores(nfull, 0, 0)
    scores(nfull + 1, 1, 1)
    update(nfull, 0)

    def two_chunks(t, c):
        scores(2 * t, 0, None)
        update(jnp.where(t == 0, nfull + 1, 2 * t - 1), 1)
        scores(2 * t + 1, 1, None)
        update(2 * t, 0)
        return c

    lax.fori_loop(0, i, two_chunks, 0)
    update(nfull + 1 - 2 * jnp.minimum(i, 1), 1)
    o0 = acc_ref[0, :V_HEAD, :] / acc_ref[0, V_HEAD:V_HEAD + 1, :]
    o1 = acc_ref[1, :V_HEAD, :] / acc_ref[1, V_HEAD:V_HEAD + 1, :]
    o_ref[0] = jnp.concatenate([o0, o1], axis=0).T.astype(BF16)


def _attention(q, k, vt):
    B, S, _ = q.shape
    tq = TQ
    npair = HEADS // 2
    return pl.pallas_call(
        _attn_kernel,
        grid=(B, npair, S // tq),
        in_specs=[pl.BlockSpec((1, tq, PAIR_W), lambda b, p, i: (b, i, p)),
                  pl.BlockSpec((1, S, PAIR_W), lambda b, p, i: (b, 0, p)),
                  pl.BlockSpec((1, S // TK, PAIR_W, TK), lambda b, p, i: (b, 0, p, 0))],
        out_specs=pl.BlockSpec((1, tq, 2 * V_HEAD), lambda b, p, i: (b, i, p)),
        out_shape=jax.ShapeDtypeStruct((B, S, HEADS * V_HEAD), BF16),
        scratch_shapes=[pltpu.VMEM((2, 1, tq), F32), pltpu.VMEM((2, HEAD_PAD, tq), F32),
                        pltpu.VMEM((2, 2, TK, tq), F32), pltpu.VMEM((2, 2, 1, tq), F32)],
        compiler_params=pltpu.CompilerParams(
            dimension_semantics=("arbitrary", "arbitrary", "arbitrary"),
            vmem_limit_bytes=VMEM_LIMIT),
        name="attention",
    )(q, k, vt)


def _ffn_body(x1, i, ng_ref, wup_ref, cw_ref, cb_ref, wdn_ref, carry_ref):
    tm = x1.shape[0]
    fc = FFN_CHUNK
    h = _rms(x1, ng_ref[...]).astype(BF16)

    @pl.when(i == 0)
    def _():
        carry_ref[...] = jnp.zeros(carry_ref.shape, F32)

    first_row = lax.broadcasted_iota(jnp.int32, (8, fc), 0) == 0

    def shift_rows(t, carry_row):
        rolled = pltpu.roll(t, 1, 0)
        head = jnp.where(first_row, carry_row, rolled[:8])
        return jnp.concatenate([head, rolled[8:]], axis=0)

    def up(c):
        return (jnp.dot(h, wup_ref[:, c * fc:(c + 1) * fc], preferred_element_type=F32),
                jnp.dot(h, wup_ref[:, D_FF + c * fc:D_FF + (c + 1) * fc],
                        preferred_element_type=F32))

    def gate(c, u, v):
        cols = slice(c * fc, (c + 1) * fc)
        w = 0.5 * cw_ref[:, cols]
        t0 = u * w[0:1]
        t1 = u * w[1:2] + shift_rows(t0, carry_ref[0:1, cols])
        hc = u * w[2:3] + shift_rows(t1, carry_ref[1:2, cols]) + 0.5 * cb_ref[:, cols]
        carry_ref[0:1, cols] = t0[tm - 1:tm]
        carry_ref[1:2, cols] = t1[tm - 1:tm]
        return ((hc + hc * jnp.tanh(hc)) * v).astype(BF16)

    acc = x1
    nxt = up(0)
    for c in range(D_FF // fc):
        u, v = nxt
        if c + 1 < D_FF // fc:
            nxt = up(c + 1)
        act = gate(c, u, v)
        acc = acc + jnp.dot(act, wdn_ref[c * fc:(c + 1) * fc, :], preferred_element_type=F32)
    return acc


def _ffn0_kernel(x_ref, u_ref, o_ref, wout_ref, ng_ref, wup_ref, cw_ref, cb_ref, wdn_ref,
                 out_ref, carry_ref):
    i = pl.program_id(1)
    mix = jnp.concatenate([u_ref[0], o_ref[0]], axis=1)
    x1 = x_ref[0] + jnp.dot(mix, wout_ref[...], preferred_element_type=F32)
    out_ref[0] = _ffn_body(x1, i, ng_ref, wup_ref, cw_ref, cb_ref, wdn_ref, carry_ref)


def _ffn1_kernel(x_ref, mg_ref, pw_ref, ps_ref, ng_ref, wup_ref, cw_ref, cb_ref, wdn_ref,
                 fg_ref, out_ref, carry_ref, hbuf_ref):
    tm = x_ref.shape[1]
    i = pl.program_id(1)
    x = x_ref[0]

    @pl.when(i == 0)
    def _():
        hbuf_ref[0:POOL_HALO, :] = jnp.zeros((POOL_HALO, D_MODEL), F32)

    hbuf_ref[POOL_HALO:POOL_HALO + tm, :] = _rms(x, mg_ref[...])
    t = i * tm + lax.broadcasted_iota(jnp.int32, (tm, 1), 0)
    ys = []
    for gi, w in enumerate(POOL_WINDOWS):
        cs = slice(gi * POOL_GROUP, (gi + 1) * POOL_GROUP)
        tot = hbuf_ref[:, cs]
        span = 1
        while span < w:
            tot = tot + pltpu.roll(tot, span, 0)
            span *= 2
        hg = hbuf_ref[POOL_HALO:POOL_HALO + tm, cs]
        cnt = jnp.minimum(t + 1, w).astype(F32)
        pooled = (tot[POOL_HALO:] / cnt - hg).astype(BF16)
        ys.append(jnp.dot(pooled, pw_ref[gi], preferred_element_type=F32))
    hbuf_ref[0:POOL_HALO, :] = hbuf_ref[tm:tm + POOL_HALO, :]
    x1 = x + jnp.concatenate(ys, axis=1) * ps_ref[...]
    x2 = _ffn_body(x1, i, ng_ref, wup_ref, cw_ref, cb_ref, wdn_ref, carry_ref)
    out_ref[0] = _rms(x2, fg_ref[...])


def _ffn_call(kernel, name, acts, consts, extra_scratch, B, S):
    tm = TM_FFN
    act_specs = [pl.BlockSpec((1, tm, a.shape[-1]), lambda b, i: (b, i, 0)) for a in acts]
    return pl.pallas_call(
        kernel,
        grid=(B, S // tm),
        in_specs=act_specs + [_const_spec(c.shape) for c in consts],
        out_specs=pl.BlockSpec((1, tm, D_MODEL), lambda b, i: (b, i, 0)),
        out_shape=jax.ShapeDtypeStruct((B, S, D_MODEL), F32),
        scratch_shapes=[pltpu.VMEM((8, D_FF), F32)] + extra_scratch,
        compiler_params=pltpu.CompilerParams(
            dimension_semantics=("arbitrary", "arbitrary"), vmem_limit_bytes=VMEM_LIMIT),
        name=name,
    )(*acts, *consts)


def _rope_tables(positions):
    inv_freq = 1.0 / (ROPE_THETA ** (jnp.arange(0, QK_ROPE, 2, dtype=F32) / QK_ROPE))
    ang = positions.astype(F32)[..., None] * inv_freq
    cos, sin = jnp.cos(ang), jnp.sin(ang)
    half = QK_ROPE // 2
    shp = cos.shape[:-1]
    ones = jnp.ones(shp + (QK_NOPE,), F32)
    zeros = lambda n: jnp.zeros(shp + (n,), F32)
    tail = HEAD_PAD - QK_NOPE - QK_ROPE
    cos_t = jnp.concatenate([ones, cos, cos, zeros(tail)], axis=-1)
    sa_t = jnp.concatenate([zeros(QK_NOPE), -sin, zeros(half + tail)], axis=-1)
    sb_t = jnp.concatenate([zeros(QK_NOPE + half), sin, zeros(tail)], axis=-1)
    return cos_t, sa_t, sb_t


def _pad_heads(w, width, take):
    K = w.shape[0]
    w = w.reshape(K, HEADS, width)[:, :, :take]
    w = jnp.pad(w, ((0, 0), (0, 0), (0, HEAD_PAD - take)))
    return w.reshape(K, HEADS * HEAD_PAD)


def kernel(x, positions, norm_mix_e, w_in, conv_w, conv_b, conv_ln_g, conv_ln_b, q_norm_g, w_uq, kv_norm_g, w_ukv, w_out, norm_mix_o, pool_w, pool_scale, norm_ffn, w_up, ffn_conv_w, ffn_conv_b, w_down, final_norm):
    B, S, _ = x.shape
    row = lambda v: v.reshape(1, -1)
    cos_t, sa_t, sb_t = _rope_tables(positions)

    c0 = 2 * CONV_CH + Q_LORA + KV_LORA
    kr_cols = jnp.pad(w_in[0][:, c0:], ((0, 0), (QK_NOPE, HEAD_PAD - QK_NOPE - QK_ROPE)))
    w_in_p = jnp.concatenate([w_in[0][:, :c0], kr_cols], axis=1).astype(BF16)
    w_uq_p = _pad_heads(w_uq[0], QK_NOPE + QK_ROPE, QK_NOPE + QK_ROPE).astype(BF16)
    w_k = _pad_heads(w_ukv[0], QK_NOPE + V_HEAD, QK_NOPE)
    w_v = w_ukv[0].reshape(KV_LORA, HEADS, QK_NOPE + V_HEAD)[:, :, QK_NOPE:].reshape(KV_LORA, -1)
    w_ukv_p = jnp.concatenate([w_k, w_v], axis=1).astype(BF16)

    u, q, k, vt = _front(x, cos_t, sa_t, sb_t, row(norm_mix_e[0]), w_in_p, row(q_norm_g[0]),
                        w_uq_p, row(kv_norm_g[0]), w_ukv_p)
    u_act = _conv_module(u, conv_w[0], row(conv_b[0]), row(conv_ln_g[0]), row(conv_ln_b[0]))
    o = _attention(q, k, vt)

    ffn_consts = lambda l: [row(norm_ffn[l]), w_up[l].astype(BF16), ffn_conv_w[l],
                            row(ffn_conv_b[l]), w_down[l].astype(BF16)]
    x = _ffn_call(_ffn0_kernel, "ffn0", [x, u_act, o],
                  [w_out[0].astype(BF16)] + ffn_consts(0), [], B, S)
    x = _ffn_call(_ffn1_kernel, "ffn1", [x],
                  [row(norm_mix_o[0]), pool_w[0].astype(BF16), row(pool_scale[0])]
                  + ffn_consts(1) + [row(final_norm)],
                  [pltpu.VMEM((POOL_HALO + TM_FFN, D_MODEL), F32)], B, S)
    return x
```

```python
import functools
import math

import jax
import jax.numpy as jnp
from jax import lax
from jax.experimental import pallas as pl
from jax.experimental.pallas import tpu as pltpu

D_MODEL = 1024
CONV_CH = 512
CONV_WIDTH = 31
HEADS = 8
QK_NOPE = 64
QK_ROPE = 32
V_HEAD = 64
Q_LORA = 384
KV_LORA = 256
ROPE_THETA = 10000.0
POOL_WINDOWS = (2, 4, 8, 16)
POOL_GROUP = 256
D_FF = 2816
FFN_CONV = 3
EPS = 1e-6

LANES = 128
HEAD_PAD = LANES
PAIR_W = 2 * HEAD_PAD
IN_COLS_PAD = 2 * CONV_CH + Q_LORA + KV_LORA + HEAD_PAD
CONV_HALO = 32
POOL_HALO = 16
FFN_CHUNK = 256
VMEM_LIMIT = 56 * 1024 * 1024

TS_FRONT = 1024
TQ = 512
TK = 256
TM_FFN = 256

BF16 = jnp.bfloat16
F32 = jnp.float32


def _rms(x, g):
    ms = jnp.mean(x * x, axis=-1, keepdims=True)
    return x * lax.rsqrt(ms + EPS) * g


def _const_spec(shape):
    nd = len(shape)
    return pl.BlockSpec(shape, lambda *_: (0,) * nd, pipeline_mode=pl.Buffered(1))


def _front_kernel(x_ref, cos_ref, sin_ref, g_ref, win_ref, qg_ref, wuq_ref,
                  kvg_ref, wukv_ref, u_ref, q_ref, k_ref, vt_ref):
    nblk = x_ref.shape[1] // TK
    c0 = 2 * CONV_CH
    scale = math.log2(math.e) / math.sqrt(QK_NOPE + QK_ROPE)
    ones = jnp.ones((V_HEAD, TK), BF16)
    first_half = lax.broadcasted_iota(jnp.int32, (TK, HEAD_PAD), 1) < QK_NOPE + QK_ROPE // 2

    def in_proj(r):
        h = _rms(x_ref[0, r * TK:(r + 1) * TK, :], g_ref[...]).astype(BF16)
        return jnp.dot(h, win_ref[...], preferred_element_type=F32)

    def latent_proj(r, proj):
        rows = slice(r * TK, (r + 1) * TK)
        u_ref[0, rows, :] = proj[:, :CONV_CH] * jax.nn.sigmoid(proj[:, CONV_CH:c0])
        cq = _rms(proj[:, c0:c0 + Q_LORA], qg_ref[...]).astype(BF16)
        ckv = _rms(proj[:, c0 + Q_LORA:c0 + Q_LORA + KV_LORA], kvg_ref[...]).astype(BF16)
        q = jnp.dot(cq, wuq_ref[...], preferred_element_type=F32)
        kv = jnp.dot(ckv, wukv_ref[...], preferred_element_type=F32)
        return q, kv, proj[:, c0 + Q_LORA + KV_LORA:]

    def finish(r, q, kv, krb):
        rows = slice(r * TK, (r + 1) * TK)
        cos, sin = cos_ref[0, rows, :], sin_ref[0, rows, :]

        def rope(t):
            partner = jnp.where(first_half, pltpu.roll(t, HEAD_PAD - QK_ROPE // 2, 1),
                                pltpu.roll(t, QK_ROPE // 2, 1))
            return t * cos + partner * sin

        kr = rope(krb)
        for hd in range(HEADS):
            sl = slice(hd * HEAD_PAD, (hd + 1) * HEAD_PAD)
            q_ref[0, rows, sl] = (rope(q[:, sl]) * scale).astype(BF16)
            k_ref[0, rows, sl] = (kv[:, sl] + kr).astype(BF16)
        vt = kv[:, HEADS * HEAD_PAD:].T.astype(BF16)
        for hd in range(HEADS):
            vt_ref[0, r, hd * HEAD_PAD:hd * HEAD_PAD + V_HEAD, :] = vt[hd * V_HEAD:(hd + 1) * V_HEAD]
            vt_ref[0, r, hd * HEAD_PAD + V_HEAD:(hd + 1) * HEAD_PAD, :] = ones

    proj = in_proj(0)
    lat = None
    for r in range(nblk):
        nxt = in_proj(r + 1) if r + 1 < nblk else None
        if lat is not None:
            finish(r - 1, *lat)
        lat = latent_proj(r, proj)
        proj = nxt
    finish(nblk - 1, *lat)


def _front(x, cos_t, sin_t, g, w_in_p, qg, w_uq_p, kvg, w_ukv_p):
    B, S, _ = x.shape
    ts = TS_FRONT
    tile = lambda w: pl.BlockSpec((1, ts, w), lambda b, i: (b, i, 0))
    return pl.pallas_call(
        _front_kernel,
        grid=(B, S // ts),
        in_specs=[tile(D_MODEL), tile(HEAD_PAD), tile(HEAD_PAD),
                  _const_spec(g.shape), _const_spec(w_in_p.shape), _const_spec(qg.shape),
                  _const_spec(w_uq_p.shape), _const_spec(kvg.shape), _const_spec(w_ukv_p.shape)],
        out_specs=[tile(CONV_CH), tile(HEADS * HEAD_PAD), tile(HEADS * HEAD_PAD),
                   pl.BlockSpec((1, ts // TK, HEADS * HEAD_PAD, TK), lambda b, i: (b, i, 0, 0))],
        out_shape=[jax.ShapeDtypeStruct((B, S, CONV_CH), F32),
                   jax.ShapeDtypeStruct((B, S, HEADS * HEAD_PAD), BF16),
                   jax.ShapeDtypeStruct((B, S, HEADS * HEAD_PAD), BF16),
                   jax.ShapeDtypeStruct((B, S // TK, HEADS * HEAD_PAD, TK), BF16)],
        compiler_params=pltpu.CompilerParams(
            dimension_semantics=("arbitrary", "arbitrary"), vmem_limit_bytes=VMEM_LIMIT),
        name="front",
    )(x, cos_t, sin_t, g, w_in_p, qg, w_uq_p, kvg, w_ukv_p)


CONV_ROWS = 32


def _conv_rows(buf_ref, r0, w_ref, b_ref, lg_ref, lb_ref):
    base = CONV_HALO - (CONV_WIDTH - 1)
    rb = CONV_ROWS
    acc = None
    for r in range(8):
        rows = rb if r == 0 else rb + 8
        z = None
        for a in range((CONV_WIDTH + base) // 8 + 1):
            k = 8 * a + r - base
            if 0 <= k < CONV_WIDTH:
                term = buf_ref[r0 + 8 * a:r0 + 8 * a + rows, :] * w_ref[k:k + 1, :]
                z = term if z is None else z + term
        if r == 0:
            acc = z + b_ref[...]
        else:
            acc = acc + pltpu.roll(z, rows - r, 0)[:rb]
    mu = jnp.mean(acc, axis=-1, keepdims=True)
    d = acc - mu
    var = jnp.mean(d * d, axis=-1, keepdims=True)
    hy = d * (lax.rsqrt(var + EPS) * 0.5) * lg_ref[...] + 0.5 * lb_ref[...]
    return (hy + hy * jnp.tanh(hy)).astype(BF16)


def _attn_kernel(q_ref, k_ref, vt_ref, o_ref, m_ref, acc_ref, s_ref, cm_ref):
    tq = q_ref.shape[1]
    i = pl.program_id(2)
    neg = jnp.finfo(F32).min
    m_ref[...] = jnp.full(m_ref.shape, neg, F32)
    acc_ref[...] = jnp.zeros(acc_ref.shape, F32)
    ndiag = tq // TK
    nfull = i * ndiag

    def scores(j, slot, diag):
        koff = pl.multiple_of(j * TK, TK)
        for hh in range(2):
            qh = q_ref[0, :, hh * HEAD_PAD:(hh + 1) * HEAD_PAD]
            kc = k_ref[0, pl.ds(koff, TK), hh * HEAD_PAD:(hh + 1) * HEAD_PAD]
            st = lax.dot_general(kc, qh, (((1,), (1,)), ((), ())),
                                 preferred_element_type=F32)
            if diag is not None:
                key = lax.broadcasted_iota(jnp.int32, (TK, tq), 0) + diag * TK
                qry = lax.broadcasted_iota(jnp.int32, (TK, tq), 1)
                st = jnp.where(key <= qry, st, neg)
            s_ref[slot, hh] = st
            cm_ref[slot, hh] = jnp.max(st, axis=0, keepdims=True)

    def update(j, slot):
        for hh in range(2):
            vt = vt_ref[0, j, hh * HEAD_PAD:(hh + 1) * HEAD_PAD, :]
            m_old = m_ref[hh]
            m_new = jnp.maximum(m_old, cm_ref[slot, hh])
            alpha = jnp.exp2(m_old - m_new)
            p = jnp.exp2(s_ref[slot, hh] - m_new)
            pv = jnp.dot(vt, p.astype(BF16), preferred_element_type=F32)
            acc_ref[hh] = alpha * acc_ref[hh] + pv
            m_ref[hh] = m_new

    assert ndiag == 2
    scores(nfull, 0, 0)
    scores(nfull + 1, 1, 1)
    update(nfull, 0)

    def two_chunks(t, c):
        scores(2 * t, 0, None)
        update(jnp.where(t == 0, nfull + 1, 2 * t - 1), 1)
        scores(2 * t + 1, 1, None)
        update(2 * t, 0)
        return c

    lax.fori_loop(0, i, two_chunks, 0)
    update(nfull + 1 - 2 * jnp.minimum(i, 1), 1)
    o0 = acc_ref[0, :V_HEAD, :] / acc_ref[0, V_HEAD:V_HEAD + 1, :]
    o1 = acc_ref[1, :V_HEAD, :] / acc_ref[1, V_HEAD:V_HEAD + 1, :]
    o_ref[0] = jnp.concatenate([o0, o1], axis=0).T.astype(BF16)


def _attention(q, k, vt):
    B, S, _ = q.shape
    tq = TQ
    npair = HEADS // 2
    return pl.pallas_call(
        _attn_kernel,
        grid=(B, npair, S // tq),
        in_specs=[pl.BlockSpec((1, tq, PAIR_W), lambda b, p, i: (b, i, p)),
                  pl.BlockSpec((1, S, PAIR_W), lambda b, p, i: (b, 0, p)),
                  pl.BlockSpec((1, S // TK, PAIR_W, TK), lambda b, p, i: (b, 0, p, 0))],
        out_specs=pl.BlockSpec((1, tq, 2 * V_HEAD), lambda b, p, i: (b, i, p)),
        out_shape=jax.ShapeDtypeStruct((B, S, HEADS * V_HEAD), BF16),
        scratch_shapes=[pltpu.VMEM((2, 1, tq), F32), pltpu.VMEM((2, HEAD_PAD, tq), F32),
                        pltpu.VMEM((2, 2, TK, tq), F32), pltpu.VMEM((2, 2, 1, tq), F32)],
        compiler_params=pltpu.CompilerParams(
            dimension_semantics=("arbitrary", "arbitrary", "arbitrary"),
            vmem_limit_bytes=VMEM_LIMIT),
        name="attention",
    )(q, k, vt)


def _ffn_body(x1, h, i, wup_ref, cw_ref, cb_ref, wdn_ref, carry_ref, side_work={}):
    tm = x1.shape[0]
    fc = FFN_CHUNK

    @pl.when(i == 0)
    def _():
        carry_ref[...] = jnp.zeros(carry_ref.shape, F32)

    first_row = lax.broadcasted_iota(jnp.int32, (8, fc), 0) == 0

    def shift_rows(t, carry_row):
        rolled = pltpu.roll(t, 1, 0)
        head = jnp.where(first_row, carry_row, rolled[:8])
        return jnp.concatenate([head, rolled[8:]], axis=0)

    def up(c):
        return (jnp.dot(h, wup_ref[:, c * fc:(c + 1) * fc], preferred_element_type=F32),
                jnp.dot(h, wup_ref[:, D_FF + c * fc:D_FF + (c + 1) * fc],
                        preferred_element_type=F32))

    def gate(c, u, v, pin):
        cols = slice(c * fc, (c + 1) * fc)
        w = 0.5 * cw_ref[:, cols]
        bias = 0.5 * cb_ref[:, cols]
        if pin is not None:
            bias = bias + pin
        t0 = u * w[0:1]
        t1 = u * w[1:2] + shift_rows(t0, carry_ref[0:1, cols])
        hc = u * w[2:3] + shift_rows(t1, carry_ref[1:2, cols]) + bias
        carry_ref[0:1, cols] = t0[tm - 1:tm]
        carry_ref[1:2, cols] = t1[tm - 1:tm]
        return ((hc + hc * jnp.tanh(hc)) * v).astype(BF16)

    acc = x1
    nxt = up(0)
    pins = {}
    for c in range(D_FF // fc):
        u, v = nxt
        if c + 1 < D_FF // fc:
            nxt = up(c + 1)
        if c in side_work:
            pins[c + 1] = side_work[c]()
        act = gate(c, u, v, pins.get(c))
        acc = acc + jnp.dot(act, wdn_ref[c * fc:(c + 1) * fc, :], preferred_element_type=F32)
    return acc


def _ffn0_kernel(x0_ref, o0_ref, u0_ref, xn_ref, on_ref, un_ref, cw31_ref, cb31_ref, lg_ref,
                 lb_ref, wout_ref, ng_ref, wup_ref, cw_ref, cb_ref, wdn_ref,
                 out_ref, carry_ref, cbuf_ref, uact_ref, x1_ref, h_ref):
    tm = out_ref.shape[1]
    i = pl.program_id(1)
    nblk = tm // CONV_ROWS

    def conv_block(u_ref, k):
        if k == 0:
            cbuf_ref[CONV_HALO:CONV_HALO + tm, :] = u_ref[0]
        blk = _conv_rows(cbuf_ref, k * CONV_ROWS, cw31_ref, cb31_ref, lg_ref, lb_ref)
        uact_ref[k * CONV_ROWS:(k + 1) * CONV_ROWS, :] = blk
        if k == nblk - 1:
            cbuf_ref[0:CONV_HALO, :] = cbuf_ref[tm:tm + CONV_HALO, :]
        bits = lax.bitcast_convert_type(blk[0:1, 0:FFN_CHUNK].astype(F32), jnp.uint32)
        half_word = jnp.uint32(16)
        return lax.shift_right_logical(lax.shift_right_logical(bits, half_word),
                                       half_word).astype(F32)

    def mix(x_ref, o_ref, slot):
        m = jnp.concatenate([uact_ref[...], o_ref[0]], axis=1)
        x1 = x_ref[0] + jnp.dot(m, wout_ref[...], preferred_element_type=F32)
        x1_ref[slot] = x1
        h_ref[slot] = _rms(x1, ng_ref[...]).astype(BF16)

    @pl.when(i == 0)
    def _():
        cbuf_ref[0:CONV_HALO, :] = jnp.zeros((CONV_HALO, CONV_CH), F32)
        for k in range(nblk):
            conv_block(u0_ref, k)
        mix(x0_ref, o0_ref, 0)

    slot = i % 2
    nchunk = D_FF // FFN_CHUNK
    stride = (nchunk - 2) // nblk
    side = {stride * k: functools.partial(conv_block, un_ref, k) for k in range(nblk)}
    side[nchunk - 2] = functools.partial(mix, xn_ref, on_ref, 1 - slot)
    out_ref[0] = _ffn_body(x1_ref[slot], h_ref[slot], i, wup_ref, cw_ref, cb_ref, wdn_ref,
                           carry_ref, side)


def _ffn1_kernel(x_ref, mg_ref, pw_ref, ps_ref, ng_ref, wup_ref, cw_ref, cb_ref, wdn_ref,
                 fg_ref, out_ref, carry_ref, hbuf_ref):
    tm = x_ref.shape[1]
    i = pl.program_id(1)
    x = x_ref[0]

    @pl.when(i == 0)
    def _():
        hbuf_ref[0:POOL_HALO, :] = jnp.zeros((POOL_HALO, D_MODEL), F32)

    hbuf_ref[POOL_HALO:POOL_HALO + tm, :] = _rms(x, mg_ref[...])
    t = i * tm + lax.broadcasted_iota(jnp.int32, (tm, 1), 0)
    ys = []
    for gi, w in enumerate(POOL_WINDOWS):
        cs = slice(gi * POOL_GROUP, (gi + 1) * POOL_GROUP)
        tot = hbuf_ref[:, cs]
        span = 1
        while span < w:
            tot = tot + pltpu.roll(tot, span, 0)
            span *= 2
        hg = hbuf_ref[POOL_HALO:POOL_HALO + tm, cs]
        cnt = jnp.minimum(t + 1, w).astype(F32)
        pooled = (tot[POOL_HALO:] / cnt - hg).astype(BF16)
        ys.append(jnp.dot(pooled, pw_ref[gi], preferred_element_type=F32))
    hbuf_ref[0:POOL_HALO, :] = hbuf_ref[tm:tm + POOL_HALO, :]
    x1 = x + jnp.concatenate(ys, axis=1) * ps_ref[...]
    h = _rms(x1, ng_ref[...]).astype(BF16)
    x2 = _ffn_body(x1, h, i, wup_ref, cw_ref, cb_ref, wdn_ref, carry_ref)
    out_ref[0] = _rms(x2, fg_ref[...])


def _ffn_call(kernel, name, acts, act_maps, consts, extra_scratch, B, S):
    tm = TM_FFN
    act_specs = [pl.BlockSpec((1, tm, a.shape[-1]), m) for a, m in zip(acts, act_maps)]
    return pl.pallas_call(
        kernel,
        grid=(B, S // tm),
        in_specs=act_specs + [_const_spec(c.shape) for c in consts],
        out_specs=pl.BlockSpec((1, tm, D_MODEL), lambda b, i: (b, i, 0)),
        out_shape=jax.ShapeDtypeStruct((B, S, D_MODEL), F32),
        scratch_shapes=[pltpu.VMEM((8, D_FF), F32)] + extra_scratch,
        compiler_params=pltpu.CompilerParams(
            dimension_semantics=("arbitrary", "arbitrary"), vmem_limit_bytes=VMEM_LIMIT),
        name=name,
    )(*acts, *consts)


def _rope_tables(positions):
    half = QK_ROPE // 2
    inv_freq = 1.0 / (ROPE_THETA ** (jnp.arange(0, QK_ROPE, 2, dtype=F32) / QK_ROPE))
    tail = jnp.zeros((HEAD_PAD - QK_NOPE - QK_ROPE,), F32)
    freq = jnp.concatenate([jnp.zeros((QK_NOPE,), F32), inv_freq, inv_freq, tail])
    sign = jnp.concatenate([jnp.zeros((QK_NOPE,), F32), -jnp.ones((half,), F32),
                            jnp.ones((half,), F32), tail])
    ang = positions.astype(F32)[..., None] * freq
    return jnp.cos(ang), jnp.sin(ang) * sign


def _pad_heads(w, width, take):
    K = w.shape[0]
    w = w.reshape(K, HEADS, width)[:, :, :take]
    w = jnp.pad(w, ((0, 0), (0, 0), (0, HEAD_PAD - take)))
    return w.reshape(K, HEADS * HEAD_PAD)


def kernel(x, positions, norm_mix_e, w_in, conv_w, conv_b, conv_ln_g, conv_ln_b, q_norm_g, w_uq, kv_norm_g, w_ukv, w_out, norm_mix_o, pool_w, pool_scale, norm_ffn, w_up, ffn_conv_w, ffn_conv_b, w_down, final_norm):
    B, S, _ = x.shape
    row = lambda v: v.reshape(1, -1)
    cos_t, sin_t = _rope_tables(positions)

    c0 = 2 * CONV_CH + Q_LORA + KV_LORA
    kr_cols = jnp.pad(w_in[0][:, c0:], ((0, 0), (QK_NOPE, HEAD_PAD - QK_NOPE - QK_ROPE)))
    w_in_p = jnp.concatenate([w_in[0][:, :c0], kr_cols], axis=1).astype(BF16)
    w_uq_p = _pad_heads(w_uq[0], QK_NOPE + QK_ROPE, QK_NOPE + QK_ROPE).astype(BF16)
    w_k = _pad_heads(w_ukv[0], QK_NOPE + V_HEAD, QK_NOPE)
    w_v = w_ukv[0].reshape(KV_LORA, HEADS, QK_NOPE + V_HEAD)[:, :, QK_NOPE:].reshape(KV_LORA, -1)
    w_ukv_p = jnp.concatenate([w_k, w_v], axis=1).astype(BF16)

    u, q, k, vt = _front(x, cos_t, sin_t, row(norm_mix_e[0]), w_in_p, row(q_norm_g[0]),
                        w_uq_p, row(kv_norm_g[0]), w_ukv_p)
    o = _attention(q, k, vt)

    ffn_consts = lambda l: [row(norm_ffn[l]), w_up[l].astype(BF16), ffn_conv_w[l],
                            row(ffn_conv_b[l]), w_down[l].astype(BF16)]
    ntile = S // TM_FFN
    this_tile = lambda b, i: (b, i, 0)
    first_tile = lambda b, i: (b, 0, 0)
    next_tile = lambda b, i: (b, jnp.minimum(i + 1, ntile - 1), 0)
    x = _ffn_call(_ffn0_kernel, "ffn0", [x, o, u, x, o, u], [first_tile] * 3 + [next_tile] * 3,
                  [conv_w[0], row(conv_b[0]), row(conv_ln_g[0]), row(conv_ln_b[0]),
                   w_out[0].astype(BF16)] + ffn_consts(0),
                  [pltpu.VMEM((CONV_HALO + TM_FFN, CONV_CH), F32),
                   pltpu.VMEM((TM_FFN, CONV_CH), BF16),
                   pltpu.VMEM((2, TM_FFN, D_MODEL), F32),
                   pltpu.VMEM((2, TM_FFN, D_MODEL), BF16)], B, S)
    x = _ffn_call(_ffn1_kernel, "ffn1", [x], [this_tile],
                  [row(norm_mix_o[0]), pool_w[0].astype(BF16), row(pool_scale[0])]
                  + ffn_consts(1) + [row(final_norm)],
                  [pltpu.VMEM((POOL_HALO + TM_FFN, D_MODEL), F32)], B, S)
    return x
```

```python
import functools
import math

import jax
import jax.numpy as jnp
from jax import lax
from jax.experimental import pallas as pl
from jax.experimental.pallas import tpu as pltpu

D_MODEL = 1024
CONV_CH = 512
CONV_WIDTH = 31
HEADS = 8
QK_NOPE = 64
QK_ROPE = 32
V_HEAD = 64
Q_LORA = 384
KV_LORA = 256
ROPE_THETA = 10000.0
POOL_WINDOWS = (2, 4, 8, 16)
POOL_GROUP = 256
D_FF = 2816
FFN_CONV = 3
EPS = 1e-6

LANES = 128
HEAD_PAD = LANES
HEADS_PER_STEP = 4
GROUP_W = HEADS_PER_STEP * HEAD_PAD
IN_COLS_PAD = 2 * CONV_CH + Q_LORA + KV_LORA + HEAD_PAD
CONV_HALO = 32
POOL_HALO = 16
FFN_CHUNK = 256
VMEM_LIMIT = 56 * 1024 * 1024

TS_FRONT = 1024
TQ = 512
TK = 256
TM_FFN = 256

BF16 = jnp.bfloat16
F32 = jnp.float32


def _rms(x, g):
    ms = jnp.mean(x * x, axis=-1, keepdims=True)
    return x * lax.rsqrt(ms + EPS) * g


def _const_spec(shape):
    nd = len(shape)
    return pl.BlockSpec(shape, lambda *_: (0,) * nd, pipeline_mode=pl.Buffered(1))


def _front_kernel(x_ref, cos_ref, sin_ref, g_ref, win_ref, qg_ref, wuq_ref,
                  kvg_ref, wukv_ref, u_ref, q_ref, k_ref, vt_ref):
    nblk = x_ref.shape[1] // TK
    c0 = 2 * CONV_CH
    scale = math.log2(math.e) / math.sqrt(QK_NOPE + QK_ROPE)
    ones = jnp.ones((V_HEAD, TK), BF16)
    first_half = lax.broadcasted_iota(jnp.int32, (TK, HEAD_PAD), 1) < QK_NOPE + QK_ROPE // 2

    def in_proj(r):
        h = _rms(x_ref[0, r * TK:(r + 1) * TK, :], g_ref[...]).astype(BF16)
        return jnp.dot(h, win_ref[...], preferred_element_type=F32)

    def latent_proj(r, proj):
        rows = slice(r * TK, (r + 1) * TK)
        u_ref[0, rows, :] = proj[:, :CONV_CH] * jax.nn.sigmoid(proj[:, CONV_CH:c0])
        cq = _rms(proj[:, c0:c0 + Q_LORA], qg_ref[...]).astype(BF16)
        ckv = _rms(proj[:, c0 + Q_LORA:c0 + Q_LORA + KV_LORA], kvg_ref[...]).astype(BF16)
        q = jnp.dot(cq, wuq_ref[...], preferred_element_type=F32)
        kv = jnp.dot(ckv, wukv_ref[...], preferred_element_type=F32)
        return q, kv, proj[:, c0 + Q_LORA + KV_LORA:]

    def finish(r, q, kv, krb):
        rows = slice(r * TK, (r + 1) * TK)
        cos, sin = cos_ref[0, rows, :], sin_ref[0, rows, :]

        def rope(t):
            partner = jnp.where(first_half, pltpu.roll(t, HEAD_PAD - QK_ROPE // 2, 1),
                                pltpu.roll(t, QK_ROPE // 2, 1))
            return t * cos + partner * sin

        kr = rope(krb)
        for hd in range(HEADS):
            sl = slice(hd * HEAD_PAD, (hd + 1) * HEAD_PAD)
            q_ref[0, rows, sl] = (rope(q[:, sl]) * scale).astype(BF16)
            k_ref[0, rows, sl] = (kv[:, sl] + kr).astype(BF16)
        vt = kv[:, HEADS * HEAD_PAD:].T.astype(BF16)
        for hd in range(HEADS):
            vt_ref[0, r, hd * HEAD_PAD:hd * HEAD_PAD + V_HEAD, :] = vt[hd * V_HEAD:(hd + 1) * V_HEAD]
            vt_ref[0, r, hd * HEAD_PAD + V_HEAD:(hd + 1) * HEAD_PAD, :] = ones

    proj = in_proj(0)
    lat = None
    for r in range(nblk):
        nxt = in_proj(r + 1) if r + 1 < nblk else None
        if lat is not None:
            finish(r - 1, *lat)
        lat = latent_proj(r, proj)
        proj = nxt
    finish(nblk - 1, *lat)


def _front(x, cos_t, sin_t, g, w_in_p, qg, w_uq_p, kvg, w_ukv_p):
    B, S, _ = x.shape
    ts = TS_FRONT
    tile = lambda w: pl.BlockSpec((1, ts, w), lambda b, i: (b, i, 0))
    return pl.pallas_call(
        _front_kernel,
        grid=(B, S // ts),
        in_specs=[tile(D_MODEL), tile(HEAD_PAD), tile(HEAD_PAD),
                  _const_spec(g.shape), _const_spec(w_in_p.shape), _const_spec(qg.shape),
                  _const_spec(w_uq_p.shape), _const_spec(kvg.shape), _const_spec(w_ukv_p.shape)],
        out_specs=[tile(CONV_CH), tile(HEADS * HEAD_PAD), tile(HEADS * HEAD_PAD),
                   pl.BlockSpec((1, ts // TK, HEADS * HEAD_PAD, TK), lambda b, i: (b, i, 0, 0))],
        out_shape=[jax.ShapeDtypeStruct((B, S, CONV_CH), F32),
                   jax.ShapeDtypeStruct((B, S, HEADS * HEAD_PAD), BF16),
                   jax.ShapeDtypeStruct((B, S, HEADS * HEAD_PAD), BF16),
                   jax.ShapeDtypeStruct((B, S // TK, HEADS * HEAD_PAD, TK), BF16)],
        compiler_params=pltpu.CompilerParams(
            dimension_semantics=("arbitrary", "arbitrary"), vmem_limit_bytes=VMEM_LIMIT),
        name="front",
    )(x, cos_t, sin_t, g, w_in_p, qg, w_uq_p, kvg, w_ukv_p)


CONV_ROWS = 32


def _conv_rows(buf_ref, r0, w_ref, b_ref, lg_ref, lb_ref):
    base = CONV_HALO - (CONV_WIDTH - 1)
    rb = CONV_ROWS
    acc = None
    for r in range(8):
        rows = rb if r == 0 else rb + 8
        z = None
        for a in range((CONV_WIDTH + base) // 8 + 1):
            k = 8 * a + r - base
            if 0 <= k < CONV_WIDTH:
                term = buf_ref[r0 + 8 * a:r0 + 8 * a + rows, :] * w_ref[k:k + 1, :]
                z = term if z is None else z + term
        if r == 0:
            acc = z + b_ref[...]
        else:
            acc = acc + pltpu.roll(z, rows - r, 0)[:rb]
    mu = jnp.mean(acc, axis=-1, keepdims=True)
    d = acc - mu
    var = jnp.mean(d * d, axis=-1, keepdims=True)
    hy = d * (lax.rsqrt(var + EPS) * 0.5) * lg_ref[...] + 0.5 * lb_ref[...]
    return (hy + hy * jnp.tanh(hy)).astype(BF16)


def _attn_kernel(q_ref, k_ref, vt_ref, o_ref, m_ref, acc_ref, s_ref, cm_ref):
    tq = q_ref.shape[1]
    i = pl.program_id(2)
    neg = jnp.finfo(F32).min
    m_ref[...] = jnp.full(m_ref.shape, neg, F32)
    acc_ref[...] = jnp.zeros(acc_ref.shape, F32)
    ndiag = tq // TK
    nfull = i * ndiag

    def scores(j, slot, diag):
        koff = pl.multiple_of(j * TK, TK)
        for hh in range(HEADS_PER_STEP):
            qh = q_ref[0, :, hh * HEAD_PAD:(hh + 1) * HEAD_PAD]
            kc = k_ref[0, pl.ds(koff, TK), hh * HEAD_PAD:(hh + 1) * HEAD_PAD]
            st = lax.dot_general(kc, qh, (((1,), (1,)), ((), ())),
                                 preferred_element_type=F32)
            if diag is not None:
                key = lax.broadcasted_iota(jnp.int32, (TK, tq), 0) + diag * TK
                qry = lax.broadcasted_iota(jnp.int32, (TK, tq), 1)
                st = jnp.where(key <= qry, st, neg)
            s_ref[slot, hh] = st
            cm_ref[slot, hh] = jnp.max(st, axis=0, keepdims=True)

    def update(j, slot):
        for hh in range(HEADS_PER_STEP):
            vt = vt_ref[0, j, hh * HEAD_PAD:(hh + 1) * HEAD_PAD, :]
            m_old = m_ref[hh]
            m_new = jnp.maximum(m_old, cm_ref[slot, hh])
            alpha = jnp.exp2(m_old - m_new)
            p = jnp.exp2(s_ref[slot, hh] - m_new)
            pv = jnp.dot(vt, p.astype(BF16), preferred_element_type=F32)
            acc_ref[hh] = alpha * acc_ref[hh] + pv
            m_ref[hh] = m_new

    assert ndiag == 2
    scores(nfull, 0, 0)
    scores(nfull + 1, 1, 1)
    update(nfull, 0)

    def two_chunks(j):
        scores(j, 0, None)
        update(jnp.where(j == 0, nfull + 1, j - 1), 1)
        scores(j + 1, 1, None)
        update(j, 0)

    def four_chunks(t, c):
        two_chunks(4 * t)
        two_chunks(4 * t + 2)
        return c

    lax.fori_loop(0, i // 2, four_chunks, 0)

    @pl.when(i % 2 == 1)
    def _():
        two_chunks(nfull - 2)

    update(nfull + 1 - 2 * jnp.minimum(i, 1), 1)
    outs = [acc_ref[hh, :V_HEAD, :] / acc_ref[hh, V_HEAD:V_HEAD + 1, :]
            for hh in range(HEADS_PER_STEP)]
    o_ref[0] = jnp.concatenate(outs, axis=0).T.astype(BF16)


def _attention(q, k, vt):
    B, S, _ = q.shape
    tq = TQ
    ngroup = HEADS // HEADS_PER_STEP
    return pl.pallas_call(
        _attn_kernel,
        grid=(B, ngroup, S // tq),
        in_specs=[pl.BlockSpec((1, tq, GROUP_W), lambda b, p, i: (b, i, p)),
                  pl.BlockSpec((1, S, GROUP_W), lambda b, p, i: (b, 0, p)),
                  pl.BlockSpec((1, S // TK, GROUP_W, TK), lambda b, p, i: (b, 0, p, 0))],
        out_specs=pl.BlockSpec((1, tq, HEADS_PER_STEP * V_HEAD), lambda b, p, i: (b, i, p)),
        out_shape=jax.ShapeDtypeStruct((B, S, HEADS * V_HEAD), BF16),
        scratch_shapes=[pltpu.VMEM((HEADS_PER_STEP, 1, tq), F32),
                        pltpu.VMEM((HEADS_PER_STEP, HEAD_PAD, tq), F32),
                        pltpu.VMEM((2, HEADS_PER_STEP, TK, tq), F32),
                        pltpu.VMEM((2, HEADS_PER_STEP, 1, tq), F32)],
        compiler_params=pltpu.CompilerParams(
            dimension_semantics=("arbitrary", "arbitrary", "arbitrary"),
            vmem_limit_bytes=VMEM_LIMIT),
        name="attention",
    )(q, k, vt)


def _ffn_body(x1, h, i, wup_ref, cw_ref, cb_ref, wdn_ref, carry_ref, side_work={}):
    tm = x1.shape[0]
    fc = FFN_CHUNK

    @pl.when(i == 0)
    def _():
        carry_ref[...] = jnp.zeros(carry_ref.shape, F32)

    first_row = lax.broadcasted_iota(jnp.int32, (8, fc), 0) == 0

    def shift_rows(t, carry_row):
        rolled = pltpu.roll(t, 1, 0)
        head = jnp.where(first_row, carry_row, rolled[:8])
        return jnp.concatenate([head, rolled[8:]], axis=0)

    def up(c):
        return (jnp.dot(h, wup_ref[:, c * fc:(c + 1) * fc], preferred_element_type=F32),
                jnp.dot(h, wup_ref[:, D_FF + c * fc:D_FF + (c + 1) * fc],
                        preferred_element_type=F32))

    def gate(c, u, v, pin):
        cols = slice(c * fc, (c + 1) * fc)
        w = 0.5 * cw_ref[:, cols]
        bias = 0.5 * cb_ref[:, cols]
        if pin is not None:
            bias = bias + pin
        t0 = u * w[0:1]
        t1 = u * w[1:2] + shift_rows(t0, carry_ref[0:1, cols])
        hc = u * w[2:3] + shift_rows(t1, carry_ref[1:2, cols]) + bias
        carry_ref[0:1, cols] = t0[tm - 1:tm]
        carry_ref[1:2, cols] = t1[tm - 1:tm]
        return ((hc + hc * jnp.tanh(hc)) * v).astype(BF16)

    acc = x1
    nxt = up(0)
    pins = {}
    for c in range(D_FF // fc):
        u, v = nxt
        if c + 1 < D_FF // fc:
            nxt = up(c + 1)
        if c in side_work:
            pins[c + 1] = side_work[c]()
        act = gate(c, u, v, pins.get(c))
        acc = acc + jnp.dot(act, wdn_ref[c * fc:(c + 1) * fc, :], preferred_element_type=F32)
    return acc


def _ffn0_kernel(x0_ref, o0_ref, u0_ref, xn_ref, on_ref, un_ref, cw31_ref, cb31_ref, lg_ref,
                 lb_ref, wout_ref, ng_ref, wup_ref, cw_ref, cb_ref, wdn_ref,
                 out_ref, carry_ref, cbuf_ref, uact_ref, x1_ref, h_ref):
    tm = out_ref.shape[1]
    i = pl.program_id(1)
    nblk = tm // CONV_ROWS

    def conv_block(u_ref, k):
        if k == 0:
            cbuf_ref[CONV_HALO:CONV_HALO + tm, :] = u_ref[0]
        blk = _conv_rows(cbuf_ref, k * CONV_ROWS, cw31_ref, cb31_ref, lg_ref, lb_ref)
        uact_ref[k * CONV_ROWS:(k + 1) * CONV_ROWS, :] = blk
        if k == nblk - 1:
            cbuf_ref[0:CONV_HALO, :] = cbuf_ref[tm:tm + CONV_HALO, :]
        bits = lax.bitcast_convert_type(blk[0:1, 0:FFN_CHUNK].astype(F32), jnp.uint32)
        half_word = jnp.uint32(16)
        return lax.shift_right_logical(lax.shift_right_logical(bits, half_word),
                                       half_word).astype(F32)

    def mix(x_ref, o_ref, slot):
        m = jnp.concatenate([uact_ref[...], o_ref[0]], axis=1)
        x1 = x_ref[0] + jnp.dot(m, wout_ref[...], preferred_element_type=F32)
        x1_ref[slot] = x1
        h_ref[slot] = _rms(x1, ng_ref[...]).astype(BF16)

    @pl.when(i == 0)
    def _():
        cbuf_ref[0:CONV_HALO, :] = jnp.zeros((CONV_HALO, CONV_CH), F32)
        for k in range(nblk):
            conv_block(u0_ref, k)
        mix(x0_ref, o0_ref, 0)

    slot = i % 2
    nchunk = D_FF // FFN_CHUNK
    stride = (nchunk - 2) // nblk
    side = {stride * k: functools.partial(conv_block, un_ref, k) for k in range(nblk)}
    side[nchunk - 2] = functools.partial(mix, xn_ref, on_ref, 1 - slot)
    out_ref[0] = _ffn_body(x1_ref[slot], h_ref[slot], i, wup_ref, cw_ref, cb_ref, wdn_ref,
                           carry_ref, side)


def _ffn1_kernel(x_ref, mg_ref, pw_ref, ps_ref, ng_ref, wup_ref, cw_ref, cb_ref, wdn_ref,
                 fg_ref, out_ref, carry_ref, hbuf_ref):
    tm = x_ref.shape[1]
    i = pl.program_id(1)
    x = x_ref[0]

    @pl.when(i == 0)
    def _():
        hbuf_ref[0:POOL_HALO, :] = jnp.zeros((POOL_HALO, D_MODEL), F32)

    hbuf_ref[POOL_HALO:POOL_HALO + tm, :] = _rms(x, mg_ref[...])
    t = i * tm + lax.broadcasted_iota(jnp.int32, (tm, 1), 0)
    ys = []
    for gi, w in enumerate(POOL_WINDOWS):
        cs = slice(gi * POOL_GROUP, (gi + 1) * POOL_GROUP)
        tot = hbuf_ref[:, cs]
        span = 1
        while span < w:
            tot = tot + pltpu.roll(tot, span, 0)
            span *= 2
        hg = hbuf_ref[POOL_HALO:POOL_HALO + tm, cs]
        cnt = jnp.minimum(t + 1, w).astype(F32)
        pooled = (tot[POOL_HALO:] / cnt - hg).astype(BF16)
        ys.append(jnp.dot(pooled, pw_ref[gi], preferred_element_type=F32))
    hbuf_ref[0:POOL_HALO, :] = hbuf_ref[tm:tm + POOL_HALO, :]
    x1 = x + jnp.concatenate(ys, axis=1) * ps_ref[...]
    h = _rms(x1, ng_ref[...]).astype(BF16)
    x2 = _ffn_body(x1, h, i, wup_ref, cw_ref, cb_ref, wdn_ref, carry_ref)
    out_ref[0] = _rms(x2, fg_ref[...])


def _ffn_call(kernel, name, acts, act_maps, consts, extra_scratch, B, S):
    tm = TM_FFN
    act_specs = [pl.BlockSpec((1, tm, a.shape[-1]), m) for a, m in zip(acts, act_maps)]
    return pl.pallas_call(
        kernel,
        grid=(B, S // tm),
        in_specs=act_specs + [_const_spec(c.shape) for c in consts],
        out_specs=pl.BlockSpec((1, tm, D_MODEL), lambda b, i: (b, i, 0)),
        out_shape=jax.ShapeDtypeStruct((B, S, D_MODEL), F32),
        scratch_shapes=[pltpu.VMEM((8, D_FF), F32)] + extra_scratch,
        compiler_params=pltpu.CompilerParams(
            dimension_semantics=("arbitrary", "arbitrary"), vmem_limit_bytes=VMEM_LIMIT),
        name=name,
    )(*acts, *consts)


def _rope_tables(positions):
    half = QK_ROPE // 2
    inv_freq = 1.0 / (ROPE_THETA ** (jnp.arange(0, QK_ROPE, 2, dtype=F32) / QK_ROPE))
    tail = jnp.zeros((HEAD_PAD - QK_NOPE - QK_ROPE,), F32)
    freq = jnp.concatenate([jnp.zeros((QK_NOPE,), F32), inv_freq, inv_freq, tail])
    sign = jnp.concatenate([jnp.zeros((QK_NOPE,), F32), -jnp.ones((half,), F32),
                            jnp.ones((half,), F32), tail])
    ang = positions.astype(F32)[..., None] * freq
    return jnp.cos(ang), jnp.sin(ang) * sign


def _pad_heads(w, width, take):
    K = w.shape[0]
    w = w.reshape(K, HEADS, width)[:, :, :take]
    w = jnp.pad(w, ((0, 0), (0, 0), (0, HEAD_PAD - take)))
    return w.reshape(K, HEADS * HEAD_PAD)


def kernel(x, positions, norm_mix_e, w_in, conv_w, conv_b, conv_ln_g, conv_ln_b, q_norm_g, w_uq, kv_norm_g, w_ukv, w_out, norm_mix_o, pool_w, pool_scale, norm_ffn, w_up, ffn_conv_w, ffn_conv_b, w_down, final_norm):
    B, S, _ = x.shape
    row = lambda v: v.reshape(1, -1)
    cos_t, sin_t = _rope_tables(positions)

    c0 = 2 * CONV_CH + Q_LORA + KV_LORA
    kr_cols = jnp.pad(w_in[0][:, c0:], ((0, 0), (QK_NOPE, HEAD_PAD - QK_NOPE - QK_ROPE)))
    w_in_p = jnp.concatenate([w_in[0][:, :c0], kr_cols], axis=1).astype(BF16)
    w_uq_p = _pad_heads(w_uq[0], QK_NOPE + QK_ROPE, QK_NOPE + QK_ROPE).astype(BF16)
    w_k = _pad_heads(w_ukv[0], QK_NOPE + V_HEAD, QK_NOPE)
    w_v = w_ukv[0].reshape(KV_LORA, HEADS, QK_NOPE + V_HEAD)[:, :, QK_NOPE:].reshape(KV_LORA, -1)
    w_ukv_p = jnp.concatenate([w_k, w_v], axis=1).astype(BF16)

    u, q, k, vt = _front(x, cos_t, sin_t, row(norm_mix_e[0]), w_in_p, row(q_norm_g[0]),
                        w_uq_p, row(kv_norm_g[0]), w_ukv_p)
    o = _attention(q, k, vt)

    ffn_consts = lambda l: [row(norm_ffn[l]), w_up[l].astype(BF16), ffn_conv_w[l],
                            row(ffn_conv_b[l]), w_down[l].astype(BF16)]
    ntile = S // TM_FFN
    this_tile = lambda b, i: (b, i, 0)
    first_tile = lambda b, i: (b, 0, 0)
    next_tile = lambda b, i: (b, jnp.minimum(i + 1, ntile - 1), 0)
    x = _ffn_call(_ffn0_kernel, "ffn0", [x, o, u, x, o, u], [first_tile] * 3 + [next_tile] * 3,
                  [conv_w[0], row(conv_b[0]), row(conv_ln_g[0]), row(conv_ln_b[0]),
                   w_out[0].astype(BF16)] + ffn_consts(0),
                  [pltpu.VMEM((CONV_HALO + TM_FFN, CONV_CH), F32),
                   pltpu.VMEM((TM_FFN, CONV_CH), BF16),
                   pltpu.VMEM((2, TM_FFN, D_MODEL), F32),
                   pltpu.VMEM((2, TM_FFN, D_MODEL), BF16)], B, S)
    x = _ffn_call(_ffn1_kernel, "ffn1", [x], [this_tile],
                  [row(norm_mix_o[0]), pool_w[0].astype(BF16), row(pool_scale[0])]
                  + ffn_consts(1) + [row(final_norm)],
                  [pltpu.VMEM((POOL_HALO + TM_FFN, D_MODEL), F32)], B, S)
    return x
```

```python
import functools
import math

import jax
import jax.numpy as jnp
from jax import lax
from jax.experimental import pallas as pl
from jax.experimental.pallas import tpu as pltpu

D_MODEL = 1024
CONV_CH = 512
CONV_WIDTH = 31
HEADS = 8
QK_NOPE = 64
QK_ROPE = 32
V_HEAD = 64
Q_LORA = 384
KV_LORA = 256
ROPE_THETA = 10000.0
POOL_WINDOWS = (2, 4, 8, 16)
POOL_GROUP = 256
D_FF = 2816
FFN_CONV = 3
EPS = 1e-6

LANES = 128
HEAD_PAD = LANES
HEADS_PER_STEP = 4
GROUP_W = HEADS_PER_STEP * HEAD_PAD
IN_COLS_PAD = 2 * CONV_CH + Q_LORA + KV_LORA + HEAD_PAD
CONV_HALO = 32
POOL_HALO = 16
FFN_CHUNK = 256
VMEM_LIMIT = 56 * 1024 * 1024

TS_FRONT = 1024
TQ = 512
TK = 256
TM_FFN = 256

BF16 = jnp.bfloat16
F32 = jnp.float32


def _rms(x, g):
    ms = jnp.mean(x * x, axis=-1, keepdims=True)
    return x * lax.rsqrt(ms + EPS) * g


def _const_spec(shape):
    nd = len(shape)
    return pl.BlockSpec(shape, lambda *_: (0,) * nd, pipeline_mode=pl.Buffered(1))


def _front_kernel(x_ref, cos_ref, sin_ref, g_ref, win_ref, qg_ref, wuq_ref,
                  kvg_ref, wukv_ref, u_ref, q_ref, k_ref, vt_ref):
    nblk = x_ref.shape[1] // TK
    c0 = 2 * CONV_CH
    scale = math.log2(math.e) / math.sqrt(QK_NOPE + QK_ROPE)
    ones = jnp.ones((V_HEAD, TK), BF16)
    first_half = lax.broadcasted_iota(jnp.int32, (TK, HEAD_PAD), 1) < QK_NOPE + QK_ROPE // 2

    def in_proj(r):
        h = _rms(x_ref[0, r * TK:(r + 1) * TK, :], g_ref[...]).astype(BF16)
        return jnp.dot(h, win_ref[...], preferred_element_type=F32)

    def latent_proj(r, proj):
        rows = slice(r * TK, (r + 1) * TK)
        u_ref[0, rows, :] = proj[:, :CONV_CH] * jax.nn.sigmoid(proj[:, CONV_CH:c0])
        cq = _rms(proj[:, c0:c0 + Q_LORA], qg_ref[...]).astype(BF16)
        ckv = _rms(proj[:, c0 + Q_LORA:c0 + Q_LORA + KV_LORA], kvg_ref[...]).astype(BF16)
        q = jnp.dot(cq, wuq_ref[...], preferred_element_type=F32)
        kv = jnp.dot(ckv, wukv_ref[...], preferred_element_type=F32)
        return q, kv, proj[:, c0 + Q_LORA + KV_LORA:]

    def finish(r, q, kv, krb):
        rows = slice(r * TK, (r + 1) * TK)
        cos, sin = cos_ref[0, rows, :], sin_ref[0, rows, :]

        def rope(t):
            partner = jnp.where(first_half, pltpu.roll(t, HEAD_PAD - QK_ROPE // 2, 1),
                                pltpu.roll(t, QK_ROPE // 2, 1))
            return t * cos + partner * sin

        kr = rope(krb)
        for hd in range(HEADS):
            sl = slice(hd * HEAD_PAD, (hd + 1) * HEAD_PAD)
            q_ref[0, rows, sl] = (rope(q[:, sl]) * scale).astype(BF16)
            k_ref[0, rows, sl] = (kv[:, sl] + kr).astype(BF16)
        vt = kv[:, HEADS * HEAD_PAD:].T.astype(BF16)
        for hd in range(HEADS):
            vt_ref[0, r, hd * HEAD_PAD:hd * HEAD_PAD + V_HEAD, :] = vt[hd * V_HEAD:(hd + 1) * V_HEAD]
            vt_ref[0, r, hd * HEAD_PAD + V_HEAD:(hd + 1) * HEAD_PAD, :] = ones

    proj = in_proj(0)
    lat = None
    for r in range(nblk):
        nxt = in_proj(r + 1) if r + 1 < nblk else None
        if lat is not None:
            finish(r - 1, *lat)
        lat = latent_proj(r, proj)
        proj = nxt
    finish(nblk - 1, *lat)


def _front(x, cos_t, sin_t, g, w_in_p, qg, w_uq_p, kvg, w_ukv_p):
    B, S, _ = x.shape
    ts = TS_FRONT
    tile = lambda w: pl.BlockSpec((1, ts, w), lambda b, i: (b, i, 0))
    return pl.pallas_call(
        _front_kernel,
        grid=(B, S // ts),
        in_specs=[tile(D_MODEL), tile(HEAD_PAD), tile(HEAD_PAD),
                  _const_spec(g.shape), _const_spec(w_in_p.shape), _const_spec(qg.shape),
                  _const_spec(w_uq_p.shape), _const_spec(kvg.shape), _const_spec(w_ukv_p.shape)],
        out_specs=[tile(CONV_CH), tile(HEADS * HEAD_PAD), tile(HEADS * HEAD_PAD),
                   pl.BlockSpec((1, ts // TK, HEADS * HEAD_PAD, TK), lambda b, i: (b, i, 0, 0))],
        out_shape=[jax.ShapeDtypeStruct((B, S, CONV_CH), F32),
                   jax.ShapeDtypeStruct((B, S, HEADS * HEAD_PAD), BF16),
                   jax.ShapeDtypeStruct((B, S, HEADS * HEAD_PAD), BF16),
                   jax.ShapeDtypeStruct((B, S // TK, HEADS * HEAD_PAD, TK), BF16)],
        compiler_params=pltpu.CompilerParams(
            dimension_semantics=("arbitrary", "arbitrary"), vmem_limit_bytes=VMEM_LIMIT),
        name="front",
    )(x, cos_t, sin_t, g, w_in_p, qg, w_uq_p, kvg, w_ukv_p)


CONV_ROWS = 32


def _conv_rows(buf_ref, r0, w_ref, b_ref, lg_ref, lb_ref):
    base = CONV_HALO - (CONV_WIDTH - 1)
    rb = CONV_ROWS
    acc = None
    for r in range(8):
        rows = rb if r == 0 else rb + 8
        z = None
        for a in range((CONV_WIDTH + base) // 8 + 1):
            k = 8 * a + r - base
            if 0 <= k < CONV_WIDTH:
                term = buf_ref[r0 + 8 * a:r0 + 8 * a + rows, :] * w_ref[k:k + 1, :]
                z = term if z is None else z + term
        if r == 0:
            acc = z + b_ref[...]
        else:
            acc = acc + pltpu.roll(z, rows - r, 0)[:rb]
    mu = jnp.mean(acc, axis=-1, keepdims=True)
    d = acc - mu
    var = jnp.mean(d * d, axis=-1, keepdims=True)
    hy = d * (lax.rsqrt(var + EPS) * 0.5) * lg_ref[...] + 0.5 * lb_ref[...]
    return (hy + hy * jnp.tanh(hy)).astype(BF16)


def _attn_kernel(q_ref, k_ref, vt_ref, o_ref, m_ref, acc_ref, s_ref, cm_ref):
    tq = q_ref.shape[1]
    i = pl.program_id(2)
    neg = jnp.finfo(F32).min
    m_ref[...] = jnp.full(m_ref.shape, neg, F32)
    acc_ref[...] = jnp.zeros(acc_ref.shape, F32)
    ndiag = tq // TK
    nfull = i * ndiag

    def scores(j, slot, diag):
        koff = pl.multiple_of(j * TK, TK)
        for hh in range(HEADS_PER_STEP):
            qh = q_ref[0, :, hh * HEAD_PAD:(hh + 1) * HEAD_PAD]
            kc = k_ref[0, pl.ds(koff, TK), hh * HEAD_PAD:(hh + 1) * HEAD_PAD]
            st = lax.dot_general(kc, qh, (((1,), (1,)), ((), ())),
                                 preferred_element_type=F32)
            if diag is not None:
                key = lax.broadcasted_iota(jnp.int32, (TK, tq), 0) + diag * TK
                qry = lax.broadcasted_iota(jnp.int32, (TK, tq), 1)
                st = jnp.where(key <= qry, st, neg)
            s_ref[slot, hh] = st
            cm_ref[slot, hh] = jnp.max(st, axis=0, keepdims=True)

    def update(j, slot):
        for hh in range(HEADS_PER_STEP):
            vt = vt_ref[0, j, hh * HEAD_PAD:(hh + 1) * HEAD_PAD, :]
            m_old = m_ref[hh]
            m_new = jnp.maximum(m_old, cm_ref[slot, hh])
            alpha = jnp.exp2(m_old - m_new)
            p = jnp.exp2(s_ref[slot, hh] - m_new)
            pv = jnp.dot(vt, p.astype(BF16), preferred_element_type=F32)
            acc_ref[hh] = alpha * acc_ref[hh] + pv
            m_ref[hh] = m_new

    assert ndiag == 2
    scores(nfull, 0, 0)
    scores(nfull + 1, 1, 1)
    update(nfull, 0)

    def two_chunks(j):
        scores(j, 0, None)
        update(jnp.where(j == 0, nfull + 1, j - 1), 1)
        scores(j + 1, 1, None)
        update(j, 0)

    def four_chunks(t, c):
        two_chunks(4 * t)
        two_chunks(4 * t + 2)
        return c

    lax.fori_loop(0, i // 2, four_chunks, 0)

    @pl.when(i % 2 == 1)
    def _():
        two_chunks(nfull - 2)

    update(nfull + 1 - 2 * jnp.minimum(i, 1), 1)
    outs = [acc_ref[hh, :V_HEAD, :] / acc_ref[hh, V_HEAD:V_HEAD + 1, :]
            for hh in range(HEADS_PER_STEP)]
    o_ref[0] = jnp.concatenate(outs, axis=0).T.astype(BF16)


def _attention(q, k, vt):
    B, S, _ = q.shape
    tq = TQ
    ngroup = HEADS // HEADS_PER_STEP
    return pl.pallas_call(
        _attn_kernel,
        grid=(B, ngroup, S // tq),
        in_specs=[pl.BlockSpec((1, tq, GROUP_W), lambda b, p, i: (b, i, p)),
                  pl.BlockSpec((1, S, GROUP_W), lambda b, p, i: (b, 0, p)),
                  pl.BlockSpec((1, S // TK, GROUP_W, TK), lambda b, p, i: (b, 0, p, 0))],
        out_specs=pl.BlockSpec((1, tq, HEADS_PER_STEP * V_HEAD), lambda b, p, i: (b, i, p)),
        out_shape=jax.ShapeDtypeStruct((B, S, HEADS * V_HEAD), BF16),
        scratch_shapes=[pltpu.VMEM((HEADS_PER_STEP, 1, tq), F32),
                        pltpu.VMEM((HEADS_PER_STEP, HEAD_PAD, tq), F32),
                        pltpu.VMEM((2, HEADS_PER_STEP, TK, tq), F32),
                        pltpu.VMEM((2, HEADS_PER_STEP, 1, tq), F32)],
        compiler_params=pltpu.CompilerParams(
            dimension_semantics=("arbitrary", "arbitrary", "arbitrary"),
            vmem_limit_bytes=VMEM_LIMIT),
        name="attention",
    )(q, k, vt)


def _ffn_body(x1, h, first, wup_ref, cw_ref, cb_ref, wdn_ref, carry_ref, side_work={}):
    tm = x1.shape[0]
    fc = FFN_CHUNK

    @pl.when(first)
    def _():
        carry_ref[...] = jnp.zeros(carry_ref.shape, F32)

    first_row = lax.broadcasted_iota(jnp.int32, (8, fc), 0) == 0

    def shift_rows(t, carry_row):
        rolled = pltpu.roll(t, 1, 0)
        head = jnp.where(first_row, carry_row, rolled[:8])
        return jnp.concatenate([head, rolled[8:]], axis=0)

    def up(c):
        return (jnp.dot(h, wup_ref[:, c * fc:(c + 1) * fc], preferred_element_type=F32),
                jnp.dot(h, wup_ref[:, D_FF + c * fc:D_FF + (c + 1) * fc],
                        preferred_element_type=F32))

    def gate(c, u, v, pin):
        cols = slice(c * fc, (c + 1) * fc)
        w = 0.5 * cw_ref[:, cols]
        bias = 0.5 * cb_ref[:, cols]
        if pin is not None:
            bias = bias + pin
        t0 = u * w[0:1]
        t1 = u * w[1:2] + shift_rows(t0, carry_ref[0:1, cols])
        hc = u * w[2:3] + shift_rows(t1, carry_ref[1:2, cols]) + bias
        carry_ref[0:1, cols] = t0[tm - 1:tm]
        carry_ref[1:2, cols] = t1[tm - 1:tm]
        return ((hc + hc * jnp.tanh(hc)) * v).astype(BF16)

    acc = x1
    nxt = up(0)
    pins = {}
    for c in range(D_FF // fc):
        u, v = nxt
        if c + 1 < D_FF // fc:
            nxt = up(c + 1)
        if c in side_work:
            pins[c + 1] = side_work[c]()
        act = gate(c, u, v, pins.get(c))
        acc = acc + jnp.dot(act, wdn_ref[c * fc:(c + 1) * fc, :], preferred_element_type=F32)
    return acc


def _zero_row(v):
    half_word = jnp.uint32(16)
    bits = lax.bitcast_convert_type(v, jnp.uint32)
    return lax.shift_right_logical(lax.shift_right_logical(bits, half_word), half_word).astype(F32)


def _ffn0_kernel(x_ref, o_ref, u_ref, cw31_ref, cb31_ref, lg_ref, lb_ref, wout_ref, ng_ref,
                 wup_ref, cw_ref, cb_ref, wdn_ref, out_ref, carry_ref, x1_ref, h_ref,
                 cbuf_ref, uact_ref):
    tm = out_ref.shape[1]
    i = pl.program_id(1)
    nblk = tm // CONV_ROWS
    nchunk = D_FF // FFN_CHUNK

    def conv_block(k):
        if k == 0:
            cbuf_ref[CONV_HALO:CONV_HALO + tm, :] = u_ref[0]
        blk = _conv_rows(cbuf_ref, k * CONV_ROWS, cw31_ref, cb31_ref, lg_ref, lb_ref)
        uact_ref[k * CONV_ROWS:(k + 1) * CONV_ROWS, :] = blk
        if k == nblk - 1:
            cbuf_ref[0:CONV_HALO, :] = cbuf_ref[tm:tm + CONV_HALO, :]
        return _zero_row(blk[0:1, 0:FFN_CHUNK].astype(F32))

    def mix(slot):
        m = jnp.concatenate([uact_ref[...], o_ref[0]], axis=1)
        x1 = x_ref[0] + jnp.dot(m, wout_ref[...], preferred_element_type=F32)
        x1_ref[slot] = x1
        h_ref[slot] = _rms(x1, ng_ref[...]).astype(BF16)

    @pl.when(i == 0)
    def _():
        cbuf_ref[0:CONV_HALO, :] = jnp.zeros((CONV_HALO, CONV_CH), F32)
        for k in range(nblk):
            conv_block(k)
        mix(0)

    @pl.when(i > 0)
    def _():
        cur = i % 2
        stride = (nchunk - 2) // nblk
        side = {stride * k: functools.partial(conv_block, k) for k in range(nblk)}
        side[nchunk - 2] = functools.partial(mix, cur)
        out_ref[0] = _ffn_body(x1_ref[1 - cur], h_ref[1 - cur], i == 1, wup_ref, cw_ref, cb_ref,
                               wdn_ref, carry_ref, side)


def _ffn1_kernel(x_ref, mg_ref, pw_ref, ps_ref, ng_ref, wup_ref, cw_ref, cb_ref, wdn_ref,
                 fg_ref, out_ref, carry_ref, x1_ref, h_ref, hbuf_ref, y_ref):
    tm = out_ref.shape[1]
    i = pl.program_id(1)
    ntile = pl.num_programs(1) - 1
    t = jnp.minimum(i, ntile - 1) * tm + lax.broadcasted_iota(jnp.int32, (tm, 1), 0)

    def norm_stage():
        hm = _rms(x_ref[0], mg_ref[...])
        hbuf_ref[POOL_HALO:POOL_HALO + tm, :] = hm
        return _zero_row(hm[0:1, 0:FFN_CHUNK])

    def group_stage(gi):
        w = POOL_WINDOWS[gi]
        cs = slice(gi * POOL_GROUP, (gi + 1) * POOL_GROUP)
        tot = hbuf_ref[:, cs]
        span = 1
        while span < w:
            tot = tot + pltpu.roll(tot, span, 0)
            span *= 2
        hg = hbuf_ref[POOL_HALO:POOL_HALO + tm, cs]
        cnt = jnp.minimum(t + 1, w).astype(F32)
        pooled = (tot[POOL_HALO:] / cnt - hg).astype(BF16)
        y = jnp.dot(pooled, pw_ref[gi], preferred_element_type=F32)
        y_ref[:, cs] = y
        return _zero_row(y[0:1, :])

    def out_stage(slot):
        hbuf_ref[0:POOL_HALO, :] = hbuf_ref[tm:tm + POOL_HALO, :]
        x1 = x_ref[0] + y_ref[...] * ps_ref[...]
        x1_ref[slot] = x1
        h_ref[slot] = _rms(x1, ng_ref[...]).astype(BF16)

    assert POOL_GROUP == FFN_CHUNK
    ngroup = len(POOL_WINDOWS)

    @pl.when(i == 0)
    def _():
        hbuf_ref[0:POOL_HALO, :] = jnp.zeros((POOL_HALO, D_MODEL), F32)
        norm_stage()
        for gi in range(ngroup):
            group_stage(gi)
        out_stage(0)

    @pl.when(i > 0)
    def _():
        cur = i % 2
        side = {0: norm_stage}
        side.update({2 + gi: functools.partial(group_stage, gi) for gi in range(ngroup)})
        side[3 + ngroup] = functools.partial(out_stage, cur)
        x2 = _ffn_body(x1_ref[1 - cur], h_ref[1 - cur], i == 1, wup_ref, cw_ref, cb_ref, wdn_ref,
                       carry_ref, side)
        out_ref[0] = _rms(x2, fg_ref[...])


def _ffn_call(kernel, name, acts, consts, extra_scratch, B, S):
    tm = TM_FFN
    ntile = S // tm
    cur = lambda b, i: (b, jnp.minimum(i, ntile - 1), 0)
    prev = lambda b, i: (b, jnp.maximum(i - 1, 0), 0)
    return pl.pallas_call(
        kernel,
        grid=(B, ntile + 1),
        in_specs=([pl.BlockSpec((1, tm, a.shape[-1]), cur) for a in acts]
                  + [_const_spec(c.shape) for c in consts]),
        out_specs=pl.BlockSpec((1, tm, D_MODEL), prev),
        out_shape=jax.ShapeDtypeStruct((B, S, D_MODEL), F32),
        scratch_shapes=[pltpu.VMEM((8, D_FF), F32), pltpu.VMEM((2, tm, D_MODEL), F32),
                        pltpu.VMEM((2, tm, D_MODEL), BF16)] + extra_scratch,
        compiler_params=pltpu.CompilerParams(
            dimension_semantics=("arbitrary", "arbitrary"), vmem_limit_bytes=VMEM_LIMIT),
        name=name,
    )(*acts, *consts)


def _rope_tables(positions):
    half = QK_ROPE // 2
    inv_freq = 1.0 / (ROPE_THETA ** (jnp.arange(0, QK_ROPE, 2, dtype=F32) / QK_ROPE))
    tail = jnp.zeros((HEAD_PAD - QK_NOPE - QK_ROPE,), F32)
    freq = jnp.concatenate([jnp.zeros((QK_NOPE,), F32), inv_freq, inv_freq, tail])
    sign = jnp.concatenate([jnp.zeros((QK_NOPE,), F32), -jnp.ones((half,), F32),
                            jnp.ones((half,), F32), tail])
    ang = positions.astype(F32)[..., None] * freq
    return jnp.cos(ang), jnp.sin(ang) * sign


def _pad_heads(w, width, take):
    K = w.shape[0]
    w = w.reshape(K, HEADS, width)[:, :, :take]
    w = jnp.pad(w, ((0, 0), (0, 0), (0, HEAD_PAD - take)))
    return w.reshape(K, HEADS * HEAD_PAD)


def kernel(x, positions, norm_mix_e, w_in, conv_w, conv_b, conv_ln_g, conv_ln_b, q_norm_g, w_uq, kv_norm_g, w_ukv, w_out, norm_mix_o, pool_w, pool_scale, norm_ffn, w_up, ffn_conv_w, ffn_conv_b, w_down, final_norm):
    B, S, _ = x.shape
    row = lambda v: v.reshape(1, -1)
    cos_t, sin_t = _rope_tables(positions)

    c0 = 2 * CONV_CH + Q_LORA + KV_LORA
    kr_cols = jnp.pad(w_in[0][:, c0:], ((0, 0), (QK_NOPE, HEAD_PAD - QK_NOPE - QK_ROPE)))
    w_in_p = jnp.concatenate([w_in[0][:, :c0], kr_cols], axis=1).astype(BF16)
    w_uq_p = _pad_heads(w_uq[0], QK_NOPE + QK_ROPE, QK_NOPE + QK_ROPE).astype(BF16)
    w_k = _pad_heads(w_ukv[0], QK_NOPE + V_HEAD, QK_NOPE)
    w_v = w_ukv[0].reshape(KV_LORA, HEADS, QK_NOPE + V_HEAD)[:, :, QK_NOPE:].reshape(KV_LORA, -1)
    w_ukv_p = jnp.concatenate([w_k, w_v], axis=1).astype(BF16)

    u, q, k, vt = _front(x, cos_t, sin_t, row(norm_mix_e[0]), w_in_p, row(q_norm_g[0]),
                        w_uq_p, row(kv_norm_g[0]), w_ukv_p)
    o = _attention(q, k, vt)

    ffn_consts = lambda l: [row(norm_ffn[l]), w_up[l].astype(BF16), ffn_conv_w[l],
                            row(ffn_conv_b[l]), w_down[l].astype(BF16)]
    x = _ffn_call(_ffn0_kernel, "ffn0", [x, o, u],
                  [conv_w[0], row(conv_b[0]), row(conv_ln_g[0]), row(conv_ln_b[0]),
                   w_out[0].astype(BF16)] + ffn_consts(0),
                  [pltpu.VMEM((CONV_HALO + TM_FFN, CONV_CH), F32),
                   pltpu.VMEM((TM_FFN, CONV_CH), BF16)], B, S)
    x = _ffn_call(_ffn1_kernel, "ffn1", [x],
                  [row(norm_mix_o[0]), pool_w[0].astype(BF16), row(pool_scale[0])]
                  + ffn_consts(1) + [row(final_norm)],
                  [pltpu.VMEM((POOL_HALO + TM_FFN, D_MODEL), F32),
                   pltpu.VMEM((TM_FFN, D_MODEL), F32)], B, S)
    return x
```

```python
import functools
import math

import jax
import jax.numpy as jnp
from jax import lax
from jax.experimental import pallas as pl
from jax.experimental.pallas import tpu as pltpu

D_MODEL = 1024
CONV_CH = 512
CONV_WIDTH = 31
HEADS = 8
QK_NOPE = 64
QK_ROPE = 32
V_HEAD = 64
Q_LORA = 384
KV_LORA = 256
ROPE_THETA = 10000.0
POOL_WINDOWS = (2, 4, 8, 16)
POOL_GROUP = 256
D_FF = 2816
FFN_CONV = 3
EPS = 1e-6

LANES = 128
HEAD_PAD = LANES
HEADS_PER_STEP = 4
GROUP_W = HEADS_PER_STEP * HEAD_PAD
IN_COLS_PAD = 2 * CONV_CH + Q_LORA + KV_LORA + HEAD_PAD
CONV_HALO = 32
POOL_HALO = 16
FFN_CHUNK = 256
VMEM_LIMIT = 56 * 1024 * 1024

TS_FRONT = 1024
TQ = 512
TK = 256
TM_FFN = 256

BF16 = jnp.bfloat16
F32 = jnp.float32


def _rms(x, g):
    ms = jnp.mean(x * x, axis=-1, keepdims=True)
    return x * lax.rsqrt(ms + EPS) * g


def _const_spec(shape):
    nd = len(shape)
    return pl.BlockSpec(shape, lambda *_: (0,) * nd, pipeline_mode=pl.Buffered(1))


def _front_kernel(x_ref, cos_ref, sin_ref, g_ref, win_ref, qg_ref, wuq_ref,
                  kvg_ref, wukv_ref, u_ref, q_ref, k_ref, vt_ref):
    nblk = x_ref.shape[1] // TK
    c0 = 2 * CONV_CH
    scale = math.log2(math.e) / math.sqrt(QK_NOPE + QK_ROPE)
    ones = jnp.ones((V_HEAD, TK), BF16)
    first_half = lax.broadcasted_iota(jnp.int32, (TK, HEAD_PAD), 1) < QK_NOPE + QK_ROPE // 2

    def in_proj(r):
        h = _rms(x_ref[0, r * TK:(r + 1) * TK, :], g_ref[...]).astype(BF16)
        return jnp.dot(h, win_ref[...], preferred_element_type=F32)

    def latent_proj(r, proj):
        rows = slice(r * TK, (r + 1) * TK)
        u_ref[0, rows, :] = proj[:, :CONV_CH] * jax.nn.sigmoid(proj[:, CONV_CH:c0])
        cq = _rms(proj[:, c0:c0 + Q_LORA], qg_ref[...]).astype(BF16)
        ckv = _rms(proj[:, c0 + Q_LORA:c0 + Q_LORA + KV_LORA], kvg_ref[...]).astype(BF16)
        q = jnp.dot(cq, wuq_ref[...], preferred_element_type=F32)
        kv = jnp.dot(ckv, wukv_ref[...], preferred_element_type=F32)
        return q, kv, proj[:, c0 + Q_LORA + KV_LORA:]

    def finish(r, q, kv, krb):
        rows = slice(r * TK, (r + 1) * TK)
        cos, sin = cos_ref[0, rows, :], sin_ref[0, rows, :]

        def rope(t):
            partner = jnp.where(first_half, pltpu.roll(t, HEAD_PAD - QK_ROPE // 2, 1),
                                pltpu.roll(t, QK_ROPE // 2, 1))
            return t * cos + partner * sin

        kr = rope(krb)
        for hd in range(HEADS):
            sl = slice(hd * HEAD_PAD, (hd + 1) * HEAD_PAD)
            q_ref[0, rows, sl] = (rope(q[:, sl]) * scale).astype(BF16)
            k_ref[0, rows, sl] = (kv[:, sl] + kr).astype(BF16)
        vt = kv[:, HEADS * HEAD_PAD:].T.astype(BF16)
        for hd in range(HEADS):
            vt_ref[0, r, hd * HEAD_PAD:hd * HEAD_PAD + V_HEAD, :] = vt[hd * V_HEAD:(hd + 1) * V_HEAD]
            vt_ref[0, r, hd * HEAD_PAD + V_HEAD:(hd + 1) * HEAD_PAD, :] = ones

    proj = in_proj(0)
    lat = None
    for r in range(nblk):
        nxt = in_proj(r + 1) if r + 1 < nblk else None
        if lat is not None:
            finish(r - 1, *lat)
        lat = latent_proj(r, proj)
        proj = nxt
    finish(nblk - 1, *lat)


def _front(x, cos_t, sin_t, g, w_in_p, qg, w_uq_p, kvg, w_ukv_p):
    B, S, _ = x.shape
    ts = TS_FRONT
    tile = lambda w: pl.BlockSpec((1, ts, w), lambda b, i: (b, i, 0))
    return pl.pallas_call(
        _front_kernel,
        grid=(B, S // ts),
        in_specs=[tile(D_MODEL), tile(HEAD_PAD), tile(HEAD_PAD),
                  _const_spec(g.shape), _const_spec(w_in_p.shape), _const_spec(qg.shape),
                  _const_spec(w_uq_p.shape), _const_spec(kvg.shape), _const_spec(w_ukv_p.shape)],
        out_specs=[tile(CONV_CH), tile(HEADS * HEAD_PAD), tile(HEADS * HEAD_PAD),
                   pl.BlockSpec((1, ts // TK, HEADS * HEAD_PAD, TK), lambda b, i: (b, i, 0, 0))],
        out_shape=[jax.ShapeDtypeStruct((B, S, CONV_CH), F32),
                   jax.ShapeDtypeStruct((B, S, HEADS * HEAD_PAD), BF16),
                   jax.ShapeDtypeStruct((B, S, HEADS * HEAD_PAD), BF16),
                   jax.ShapeDtypeStruct((B, S // TK, HEADS * HEAD_PAD, TK), BF16)],
        compiler_params=pltpu.CompilerParams(
            dimension_semantics=("arbitrary", "arbitrary"), vmem_limit_bytes=VMEM_LIMIT),
        name="front",
    )(x, cos_t, sin_t, g, w_in_p, qg, w_uq_p, kvg, w_ukv_p)


CONV_ROWS = 32


def _conv_shifted_copies(cbuf_ref):
    buf = cbuf_ref[0]
    n = buf.shape[0]
    for r in range(1, 8):
        moved = pltpu.roll(buf, n - r, 0)
        cbuf_ref[r] = moved
    return moved


def _conv_rows(cbuf_ref, r0, w_ref, b_ref, lg_ref, lb_ref):
    base = CONV_HALO - (CONV_WIDTH - 1)
    acc = b_ref[...]
    for k in range(CONV_WIDTH):
        a, r = divmod(k + base, 8)
        acc = acc + cbuf_ref[r, r0 + 8 * a:r0 + 8 * a + CONV_ROWS, :] * w_ref[k:k + 1, :]
    mu = jnp.mean(acc, axis=-1, keepdims=True)
    d = acc - mu
    var = jnp.mean(d * d, axis=-1, keepdims=True)
    hy = d * (lax.rsqrt(var + EPS) * 0.5) * lg_ref[...] + 0.5 * lb_ref[...]
    return (hy + hy * jnp.tanh(hy)).astype(BF16)


def _attn_kernel(q_ref, k_ref, vt_ref, o_ref, m_ref, acc_ref, s_ref, cm_ref):
    tq = q_ref.shape[1]
    i = pl.program_id(2)
    neg = jnp.finfo(F32).min
    m_ref[...] = jnp.full(m_ref.shape, neg, F32)
    acc_ref[...] = jnp.zeros(acc_ref.shape, F32)
    ndiag = tq // TK
    nfull = i * ndiag

    def scores(j, slot, diag):
        koff = pl.multiple_of(j * TK, TK)
        for hh in range(HEADS_PER_STEP):
            qh = q_ref[0, :, hh * HEAD_PAD:(hh + 1) * HEAD_PAD]
            kc = k_ref[0, pl.ds(koff, TK), hh * HEAD_PAD:(hh + 1) * HEAD_PAD]
            st = lax.dot_general(kc, qh, (((1,), (1,)), ((), ())),
                                 preferred_element_type=F32)
            if diag is not None:
                key = lax.broadcasted_iota(jnp.int32, (TK, tq), 0) + diag * TK
                qry = lax.broadcasted_iota(jnp.int32, (TK, tq), 1)
                st = jnp.where(key <= qry, st, neg)
            s_ref[slot, hh] = st
            cm_ref[slot, hh] = jnp.max(st, axis=0, keepdims=True)

    def update(j, slot):
        for hh in range(HEADS_PER_STEP):
            vt = vt_ref[0, j, hh * HEAD_PAD:(hh + 1) * HEAD_PAD, :]
            m_old = m_ref[hh]
            m_new = jnp.maximum(m_old, cm_ref[slot, hh])
            alpha = jnp.exp2(m_old - m_new)
            p = jnp.exp2(s_ref[slot, hh] - m_new)
            pv = jnp.dot(vt, p.astype(BF16), preferred_element_type=F32)
            acc_ref[hh] = alpha * acc_ref[hh] + pv
            m_ref[hh] = m_new

    assert ndiag == 2
    scores(nfull, 0, 0)
    scores(nfull + 1, 1, 1)
    update(nfull, 0)

    def two_chunks(j):
        scores(j, 0, None)
        update(jnp.where(j == 0, nfull + 1, j - 1), 1)
        scores(j + 1, 1, None)
        update(j, 0)

    def four_chunks(t, c):
        two_chunks(4 * t)
        two_chunks(4 * t + 2)
        return c

    lax.fori_loop(0, i // 2, four_chunks, 0)

    @pl.when(i % 2 == 1)
    def _():
        two_chunks(nfull - 2)

    update(nfull + 1 - 2 * jnp.minimum(i, 1), 1)
    outs = [acc_ref[hh, :V_HEAD, :] / acc_ref[hh, V_HEAD:V_HEAD + 1, :]
            for hh in range(HEADS_PER_STEP)]
    o_ref[0] = jnp.concatenate(outs, axis=0).T.astype(BF16)


def _attention(q, k, vt):
    B, S, _ = q.shape
    tq = TQ
    ngroup = HEADS // HEADS_PER_STEP
    return pl.pallas_call(
        _attn_kernel,
        grid=(B, ngroup, S // tq),
        in_specs=[pl.BlockSpec((1, tq, GROUP_W), lambda b, p, i: (b, i, p)),
                  pl.BlockSpec((1, S, GROUP_W), lambda b, p, i: (b, 0, p)),
                  pl.BlockSpec((1, S // TK, GROUP_W, TK), lambda b, p, i: (b, 0, p, 0))],
        out_specs=pl.BlockSpec((1, tq, HEADS_PER_STEP * V_HEAD), lambda b, p, i: (b, i, p)),
        out_shape=jax.ShapeDtypeStruct((B, S, HEADS * V_HEAD), BF16),
        scratch_shapes=[pltpu.VMEM((HEADS_PER_STEP, 1, tq), F32),
                        pltpu.VMEM((HEADS_PER_STEP, HEAD_PAD, tq), F32),
                        pltpu.VMEM((2, HEADS_PER_STEP, TK, tq), F32),
                        pltpu.VMEM((2, HEADS_PER_STEP, 1, tq), F32)],
        compiler_params=pltpu.CompilerParams(
            dimension_semantics=("arbitrary", "arbitrary", "arbitrary"),
            vmem_limit_bytes=VMEM_LIMIT),
        name="attention",
    )(q, k, vt)


def _ffn_body(x1, h, first, wup_ref, cw_ref, cb_ref, wdn_ref, carry_ref, side_work={}):
    tm = x1.shape[0]
    fc = FFN_CHUNK

    @pl.when(first)
    def _():
        carry_ref[...] = jnp.zeros(carry_ref.shape, F32)

    first_row = lax.broadcasted_iota(jnp.int32, (8, fc), 0) == 0

    def shift_rows(t, carry_row):
        rolled = pltpu.roll(t, 1, 0)
        head = jnp.where(first_row, carry_row, rolled[:8])
        return jnp.concatenate([head, rolled[8:]], axis=0)

    def up(c):
        return (jnp.dot(h, wup_ref[:, c * fc:(c + 1) * fc], preferred_element_type=F32),
                jnp.dot(h, wup_ref[:, D_FF + c * fc:D_FF + (c + 1) * fc],
                        preferred_element_type=F32))

    def gate(c, u, v, pin):
        cols = slice(c * fc, (c + 1) * fc)
        w = 0.5 * cw_ref[:, cols]
        bias = 0.5 * cb_ref[:, cols]
        if pin is not None:
            bias = bias + pin
        t0 = u * w[0:1]
        t1 = u * w[1:2] + shift_rows(t0, carry_ref[0:1, cols])
        hc = u * w[2:3] + shift_rows(t1, carry_ref[1:2, cols]) + bias
        carry_ref[0:1, cols] = t0[tm - 1:tm]
        carry_ref[1:2, cols] = t1[tm - 1:tm]
        return ((hc + hc * jnp.tanh(hc)) * v).astype(BF16)

    acc = x1
    nxt = up(0)
    pins = {}
    for c in range(D_FF // fc):
        u, v = nxt
        if c + 1 < D_FF // fc:
            nxt = up(c + 1)
        if c in side_work:
            pins[c + 1] = side_work[c]()
        act = gate(c, u, v, pins.get(c))
        acc = acc + jnp.dot(act, wdn_ref[c * fc:(c + 1) * fc, :], preferred_element_type=F32)
    return acc


def _zero_row(v):
    half_word = jnp.uint32(16)
    bits = lax.bitcast_convert_type(v, jnp.uint32)
    return lax.shift_right_logical(lax.shift_right_logical(bits, half_word), half_word).astype(F32)


def _ffn0_kernel(x_ref, o_ref, u_ref, cw31_ref, cb31_ref, lg_ref, lb_ref, wout_ref, ng_ref,
                 wup_ref, cw_ref, cb_ref, wdn_ref, out_ref, carry_ref, x1_ref, h_ref,
                 cbuf_ref, uact_ref):
    tm = out_ref.shape[1]
    i = pl.program_id(1)
    nblk = tm // CONV_ROWS
    nchunk = D_FF // FFN_CHUNK

    def conv_load():
        cbuf_ref[0, CONV_HALO:CONV_HALO + tm, :] = u_ref[0]
        return _zero_row(_conv_shifted_copies(cbuf_ref)[0:1, 0:FFN_CHUNK])

    def conv_block(k):
        blk = _conv_rows(cbuf_ref, k * CONV_ROWS, cw31_ref, cb31_ref, lg_ref, lb_ref)
        uact_ref[k * CONV_ROWS:(k + 1) * CONV_ROWS, :] = blk
        if k == nblk - 1:
            cbuf_ref[0, 0:CONV_HALO, :] = cbuf_ref[0, tm:tm + CONV_HALO, :]
        return _zero_row(blk[0:1, 0:FFN_CHUNK].astype(F32))

    def mix(slot):
        m = jnp.concatenate([uact_ref[...], o_ref[0]], axis=1)
        x1 = x_ref[0] + jnp.dot(m, wout_ref[...], preferred_element_type=F32)
        x1_ref[slot] = x1
        h_ref[slot] = _rms(x1, ng_ref[...]).astype(BF16)

    @pl.when(i == 0)
    def _():
        cbuf_ref[0, 0:CONV_HALO, :] = jnp.zeros((CONV_HALO, CONV_CH), F32)
        conv_load()
        for k in range(nblk):
            conv_block(k)
        mix(0)

    @pl.when(i > 0)
    def _():
        cur = i % 2
        assert nblk + 2 <= nchunk - 1
        side = {0: conv_load}
        side.update({1 + k: functools.partial(conv_block, k) for k in range(nblk)})
        side[nchunk - 2] = functools.partial(mix, cur)
        out_ref[0] = _ffn_body(x1_ref[1 - cur], h_ref[1 - cur], i == 1, wup_ref, cw_ref, cb_ref,
                               wdn_ref, carry_ref, side)


def _ffn1_kernel(x_ref, mg_ref, pw_ref, ps_ref, ng_ref, wup_ref, cw_ref, cb_ref, wdn_ref,
                 fg_ref, out_ref, carry_ref, x1_ref, h_ref, hbuf_ref, y_ref):
    tm = out_ref.shape[1]
    i = pl.program_id(1)
    ntile = pl.num_programs(1) - 1
    t = jnp.minimum(i, ntile - 1) * tm + lax.broadcasted_iota(jnp.int32, (tm, 1), 0)

    def norm_stage():
        hm = _rms(x_ref[0], mg_ref[...])
        hbuf_ref[POOL_HALO:POOL_HALO + tm, :] = hm
        return _zero_row(hm[0:1, 0:FFN_CHUNK])

    def group_stage(gi):
        w = POOL_WINDOWS[gi]
        cs = slice(gi * POOL_GROUP, (gi + 1) * POOL_GROUP)
        tot = hbuf_ref[:, cs]
        span = 1
        while span < w:
            tot = tot + pltpu.roll(tot, span, 0)
            span *= 2
        hg = hbuf_ref[POOL_HALO:POOL_HALO + tm, cs]
        cnt = jnp.minimum(t + 1, w).astype(F32)
        pooled = (tot[POOL_HALO:] / cnt - hg).astype(BF16)
        y = jnp.dot(pooled, pw_ref[gi], preferred_element_type=F32)
        y_ref[:, cs] = y
        return _zero_row(y[0:1, :])

    def out_stage(slot):
        hbuf_ref[0:POOL_HALO, :] = hbuf_ref[tm:tm + POOL_HALO, :]
        x1 = x_ref[0] + y_ref[...] * ps_ref[...]
        x1_ref[slot] = x1
        h_ref[slot] = _rms(x1, ng_ref[...]).astype(BF16)

    assert POOL_GROUP == FFN_CHUNK
    ngroup = len(POOL_WINDOWS)

    @pl.when(i == 0)
    def _():
        hbuf_ref[0:POOL_HALO, :] = jnp.zeros((POOL_HALO, D_MODEL), F32)
        norm_stage()
        for gi in range(ngroup):
            group_stage(gi)
        out_stage(0)

    @pl.when(i > 0)
    def _():
        cur = i % 2
        side = {0: norm_stage}
        side.update({2 + gi: functools.partial(group_stage, gi) for gi in range(ngroup)})
        side[3 + ngroup] = functools.partial(out_stage, cur)
        x2 = _ffn_body(x1_ref[1 - cur], h_ref[1 - cur], i == 1, wup_ref, cw_ref, cb_ref, wdn_ref,
                       carry_ref, side)
        out_ref[0] = _rms(x2, fg_ref[...])


def _ffn_call(kernel, name, acts, consts, extra_scratch, B, S):
    tm = TM_FFN
    ntile = S // tm
    cur = lambda b, i: (b, jnp.minimum(i, ntile - 1), 0)
    prev = lambda b, i: (b, jnp.maximum(i - 1, 0), 0)
    return pl.pallas_call(
        kernel,
        grid=(B, ntile + 1),
        in_specs=([pl.BlockSpec((1, tm, a.shape[-1]), cur) for a in acts]
                  + [_const_spec(c.shape) for c in consts]),
        out_specs=pl.BlockSpec((1, tm, D_MODEL), prev),
        out_shape=jax.ShapeDtypeStruct((B, S, D_MODEL), F32),
        scratch_shapes=[pltpu.VMEM((8, D_FF), F32), pltpu.VMEM((2, tm, D_MODEL), F32),
                        pltpu.VMEM((2, tm, D_MODEL), BF16)] + extra_scratch,
        compiler_params=pltpu.CompilerParams(
            dimension_semantics=("arbitrary", "arbitrary"), vmem_limit_bytes=VMEM_LIMIT),
        name=name,
    )(*acts, *consts)


def _rope_tables(positions):
    half = QK_ROPE // 2
    inv_freq = 1.0 / (ROPE_THETA ** (jnp.arange(0, QK_ROPE, 2, dtype=F32) / QK_ROPE))
    tail = jnp.zeros((HEAD_PAD - QK_NOPE - QK_ROPE,), F32)
    freq = jnp.concatenate([jnp.zeros((QK_NOPE,), F32), inv_freq, inv_freq, tail])
    sign = jnp.concatenate([jnp.zeros((QK_NOPE,), F32), -jnp.ones((half,), F32),
                            jnp.ones((half,), F32), tail])
    ang = positions.astype(F32)[..., None] * freq
    return jnp.cos(ang), jnp.sin(ang) * sign


def _pad_heads(w, width, take):
    K = w.shape[0]
    w = w.reshape(K, HEADS, width)[:, :, :take]
    w = jnp.pad(w, ((0, 0), (0, 0), (0, HEAD_PAD - take)))
    return w.reshape(K, HEADS * HEAD_PAD)


def kernel(x, positions, norm_mix_e, w_in, conv_w, conv_b, conv_ln_g, conv_ln_b, q_norm_g, w_uq, kv_norm_g, w_ukv, w_out, norm_mix_o, pool_w, pool_scale, norm_ffn, w_up, ffn_conv_w, ffn_conv_b, w_down, final_norm):
    B, S, _ = x.shape
    row = lambda v: v.reshape(1, -1)
    cos_t, sin_t = _rope_tables(positions)

    c0 = 2 * CONV_CH + Q_LORA + KV_LORA
    kr_cols = jnp.pad(w_in[0][:, c0:], ((0, 0), (QK_NOPE, HEAD_PAD - QK_NOPE - QK_ROPE)))
    w_in_p = jnp.concatenate([w_in[0][:, :c0], kr_cols], axis=1).astype(BF16)
    w_uq_p = _pad_heads(w_uq[0], QK_NOPE + QK_ROPE, QK_NOPE + QK_ROPE).astype(BF16)
    w_k = _pad_heads(w_ukv[0], QK_NOPE + V_HEAD, QK_NOPE)
    w_v = w_ukv[0].reshape(KV_LORA, HEADS, QK_NOPE + V_HEAD)[:, :, QK_NOPE:].reshape(KV_LORA, -1)
    w_ukv_p = jnp.concatenate([w_k, w_v], axis=1).astype(BF16)

    u, q, k, vt = _front(x, cos_t, sin_t, row(norm_mix_e[0]), w_in_p, row(q_norm_g[0]),
                        w_uq_p, row(kv_norm_g[0]), w_ukv_p)
    o = _attention(q, k, vt)

    ffn_consts = lambda l: [row(norm_ffn[l]), w_up[l].astype(BF16), ffn_conv_w[l],
                            row(ffn_conv_b[l]), w_down[l].astype(BF16)]
    x = _ffn_call(_ffn0_kernel, "ffn0", [x, o, u],
                  [conv_w[0], row(conv_b[0]), row(conv_ln_g[0]), row(conv_ln_b[0]),
                   w_out[0].astype(BF16)] + ffn_consts(0),
                  [pltpu.VMEM((8, CONV_HALO + TM_FFN, CONV_CH), F32),
                   pltpu.VMEM((TM_FFN, CONV_CH), BF16)], B, S)
    x = _ffn_call(_ffn1_kernel, "ffn1", [x],
                  [row(norm_mix_o[0]), pool_w[0].astype(BF16), row(pool_scale[0])]
                  + ffn_consts(1) + [row(final_norm)],
                  [pltpu.VMEM((POOL_HALO + TM_FFN, D_MODEL), F32),
                   pltpu.VMEM((TM_FFN, D_MODEL), F32)], B, S)
    return x
```

```python
import functools
import math

import jax
import jax.numpy as jnp
from jax import lax
from jax.experimental import pallas as pl
from jax.experimental.pallas import tpu as pltpu

D_MODEL = 1024
CONV_CH = 512
CONV_WIDTH = 31
HEADS = 8
QK_NOPE = 64
QK_ROPE = 32
V_HEAD = 64
Q_LORA = 384
KV_LORA = 256
ROPE_THETA = 10000.0
POOL_WINDOWS = (2, 4, 8, 16)
POOL_GROUP = 256
D_FF = 2816
FFN_CONV = 3
EPS = 1e-6

LANES = 128
HEAD_PAD = LANES
HEADS_PER_STEP = 4
GROUP_W = HEADS_PER_STEP * HEAD_PAD
IN_COLS_PAD = 2 * CONV_CH + Q_LORA + KV_LORA + HEAD_PAD
CONV_HALO = 32
POOL_HALO = 16
FFN_CHUNK = 256
VMEM_LIMIT = 56 * 1024 * 1024

TS_FRONT = 1024
TQ = 512
TK = 256
TM_FFN = 256
ROPE_PACK = LANES // QK_ROPE
ROPE_ROWS = TK // ROPE_PACK

BF16 = jnp.bfloat16
F32 = jnp.float32


def _rms(x, g):
    ms = jnp.mean(x * x, axis=-1, keepdims=True)
    return x * lax.rsqrt(ms + EPS) * g


def _const_spec(shape):
    nd = len(shape)
    return pl.BlockSpec(shape, lambda *_: (0,) * nd, pipeline_mode=pl.Buffered(1))


def _front_kernel(x_ref, cos_ref, sin_ref, g_ref, win_ref, qg_ref, wuq_ref,
                  kvg_ref, wukv_ref, u_ref, q_ref, k_ref, vt_ref):
    nblk = x_ref.shape[1] // TK
    c0 = 2 * CONV_CH
    scale = math.log2(math.e) / math.sqrt(QK_NOPE + QK_ROPE)
    ones = jnp.ones((V_HEAD, TK), BF16)
    first_half = lax.broadcasted_iota(jnp.int32, (TK, HEAD_PAD), 1) < QK_NOPE + QK_ROPE // 2
    lane = lax.broadcasted_iota(jnp.int32, (ROPE_ROWS, HEAD_PAD), 1)
    rope_lanes = (lane >= QK_NOPE) & (lane < QK_NOPE + QK_ROPE)

    def unpack_table(packed, fill):
        pieces = []
        for g in range(ROPE_PACK):
            shift = (QK_NOPE - QK_ROPE * g) % HEAD_PAD
            moved = pltpu.roll(packed, shift, 1) if shift else packed
            pieces.append(jnp.where(rope_lanes, moved, fill))
        return jnp.concatenate(pieces, axis=0)

    def in_proj(r):
        h = _rms(x_ref[0, r * TK:(r + 1) * TK, :], g_ref[...]).astype(BF16)
        return jnp.dot(h, win_ref[...], preferred_element_type=F32)

    def latent_proj(r, proj):
        rows = slice(r * TK, (r + 1) * TK)
        u_ref[0, rows, :] = proj[:, :CONV_CH] * jax.nn.sigmoid(proj[:, CONV_CH:c0])
        cq = _rms(proj[:, c0:c0 + Q_LORA], qg_ref[...]).astype(BF16)
        ckv = _rms(proj[:, c0 + Q_LORA:c0 + Q_LORA + KV_LORA], kvg_ref[...]).astype(BF16)
        q = jnp.dot(cq, wuq_ref[...], preferred_element_type=F32)
        kv = jnp.dot(ckv, wukv_ref[...], preferred_element_type=F32)
        return q, kv, proj[:, c0 + Q_LORA + KV_LORA:]

    def finish(r, q, kv, krb):
        rows = slice(r * TK, (r + 1) * TK)
        cos, sin = unpack_table(cos_ref[0, r], 1.0), unpack_table(sin_ref[0, r], 0.0)

        def rope(t):
            partner = jnp.where(first_half, pltpu.roll(t, HEAD_PAD - QK_ROPE // 2, 1),
                                pltpu.roll(t, QK_ROPE // 2, 1))
            return t * cos + partner * sin

        kr = rope(krb)
        for hd in range(HEADS):
            sl = slice(hd * HEAD_PAD, (hd + 1) * HEAD_PAD)
            q_ref[0, rows, sl] = (rope(q[:, sl]) * scale).astype(BF16)
            k_ref[0, rows, sl] = (kv[:, sl] + kr).astype(BF16)
        vt = kv[:, HEADS * HEAD_PAD:].T.astype(BF16)
        for hd in range(HEADS):
            vt_ref[0, r, hd * HEAD_PAD:hd * HEAD_PAD + V_HEAD, :] = vt[hd * V_HEAD:(hd + 1) * V_HEAD]
            vt_ref[0, r, hd * HEAD_PAD + V_HEAD:(hd + 1) * HEAD_PAD, :] = ones

    proj = in_proj(0)
    lat = None
    for r in range(nblk):
        nxt = in_proj(r + 1) if r + 1 < nblk else None
        if lat is not None:
            finish(r - 1, *lat)
        lat = latent_proj(r, proj)
        proj = nxt
    finish(nblk - 1, *lat)


def _front(x, cos_t, sin_t, g, w_in_p, qg, w_uq_p, kvg, w_ukv_p):
    B, S, _ = x.shape
    ts = TS_FRONT
    tile = lambda w: pl.BlockSpec((1, ts, w), lambda b, i: (b, i, 0))
    table = pl.BlockSpec((1, ts // TK, ROPE_ROWS, HEAD_PAD), lambda b, i: (b, i, 0, 0))
    return pl.pallas_call(
        _front_kernel,
        grid=(B, S // ts),
        in_specs=[tile(D_MODEL), table, table,
                  _const_spec(g.shape), _const_spec(w_in_p.shape), _const_spec(qg.shape),
                  _const_spec(w_uq_p.shape), _const_spec(kvg.shape), _const_spec(w_ukv_p.shape)],
        out_specs=[tile(CONV_CH), tile(HEADS * HEAD_PAD), tile(HEADS * HEAD_PAD),
                   pl.BlockSpec((1, ts // TK, HEADS * HEAD_PAD, TK), lambda b, i: (b, i, 0, 0))],
        out_shape=[jax.ShapeDtypeStruct((B, S, CONV_CH), F32),
                   jax.ShapeDtypeStruct((B, S, HEADS * HEAD_PAD), BF16),
                   jax.ShapeDtypeStruct((B, S, HEADS * HEAD_PAD), BF16),
                   jax.ShapeDtypeStruct((B, S // TK, HEADS * HEAD_PAD, TK), BF16)],
        compiler_params=pltpu.CompilerParams(
            dimension_semantics=("arbitrary", "arbitrary"), vmem_limit_bytes=VMEM_LIMIT),
        name="front",
    )(x, cos_t, sin_t, g, w_in_p, qg, w_uq_p, kvg, w_ukv_p)


CONV_ROWS = 32


def _conv_shifted_copies(cbuf_ref):
    buf = cbuf_ref[0]
    n = buf.shape[0]
    for r in range(1, 8):
        moved = pltpu.roll(buf, n - r, 0)
        cbuf_ref[r] = moved
    return moved


def _conv_rows(cbuf_ref, r0, w_ref, b_ref, lg_ref, lb_ref):
    base = CONV_HALO - (CONV_WIDTH - 1)
    acc = b_ref[...]
    for k in range(CONV_WIDTH):
        a, r = divmod(k + base, 8)
        acc = acc + cbuf_ref[r, r0 + 8 * a:r0 + 8 * a + CONV_ROWS, :] * w_ref[k:k + 1, :]
    mu = jnp.mean(acc, axis=-1, keepdims=True)
    d = acc - mu
    var = jnp.mean(d * d, axis=-1, keepdims=True)
    hy = d * (lax.rsqrt(var + EPS) * 0.5) * lg_ref[...] + 0.5 * lb_ref[...]
    return (hy + hy * jnp.tanh(hy)).astype(BF16)


def _attn_kernel(q_ref, k_ref, vt_ref, o_ref, m_ref, acc_ref, s_ref, cm_ref):
    tq = q_ref.shape[1]
    i = pl.program_id(2)
    neg = jnp.finfo(F32).min
    m_ref[...] = jnp.full(m_ref.shape, neg, F32)
    acc_ref[...] = jnp.zeros(acc_ref.shape, F32)
    ndiag = tq // TK
    nfull = i * ndiag

    def scores(j, slot, diag):
        koff = pl.multiple_of(j * TK, TK)
        for hh in range(HEADS_PER_STEP):
            qh = q_ref[0, :, hh * HEAD_PAD:(hh + 1) * HEAD_PAD]
            kc = k_ref[0, pl.ds(koff, TK), hh * HEAD_PAD:(hh + 1) * HEAD_PAD]
            st = lax.dot_general(kc, qh, (((1,), (1,)), ((), ())),
                                 preferred_element_type=F32)
            if diag is not None:
                key = lax.broadcasted_iota(jnp.int32, (TK, tq), 0) + diag * TK
                qry = lax.broadcasted_iota(jnp.int32, (TK, tq), 1)
                st = jnp.where(key <= qry, st, neg)
            s_ref[slot, hh] = st
            cm_ref[slot, hh] = jnp.max(st, axis=0, keepdims=True)

    def update(j, slot):
        for hh in range(HEADS_PER_STEP):
            vt = vt_ref[0, j, hh * HEAD_PAD:(hh + 1) * HEAD_PAD, :]
            m_old = m_ref[hh]
            m_new = jnp.maximum(m_old, cm_ref[slot, hh])
            alpha = jnp.exp2(m_old - m_new)
            p = jnp.exp2(s_ref[slot, hh] - m_new)
            pv = jnp.dot(vt, p.astype(BF16), preferred_element_type=F32)
            acc_ref[hh] = alpha * acc_ref[hh] + pv
            m_ref[hh] = m_new

    assert ndiag == 2
    scores(nfull, 0, 0)
    scores(nfull + 1, 1, 1)
    update(nfull, 0)

    def two_chunks(j):
        scores(j, 0, None)
        update(jnp.where(j == 0, nfull + 1, j - 1), 1)
        scores(j + 1, 1, None)
        update(j, 0)

    def four_chunks(t, c):
        two_chunks(4 * t)
        two_chunks(4 * t + 2)
        return c

    lax.fori_loop(0, i // 2, four_chunks, 0)

    @pl.when(i % 2 == 1)
    def _():
        two_chunks(nfull - 2)

    update(nfull + 1 - 2 * jnp.minimum(i, 1), 1)
    outs = [acc_ref[hh, :V_HEAD, :] / acc_ref[hh, V_HEAD:V_HEAD + 1, :]
            for hh in range(HEADS_PER_STEP)]
    o_ref[0] = jnp.concatenate(outs, axis=0).T.astype(BF16)


def _attention(q, k, vt):
    B, S, _ = q.shape
    tq = TQ
    ngroup = HEADS // HEADS_PER_STEP
    return pl.pallas_call(
        _attn_kernel,
        grid=(B, ngroup, S // tq),
        in_specs=[pl.BlockSpec((1, tq, GROUP_W), lambda b, p, i: (b, i, p)),
                  pl.BlockSpec((1, S, GROUP_W), lambda b, p, i: (b, 0, p)),
                  pl.BlockSpec((1, S // TK, GROUP_W, TK), lambda b, p, i: (b, 0, p, 0))],
        out_specs=pl.BlockSpec((1, tq, HEADS_PER_STEP * V_HEAD), lambda b, p, i: (b, i, p)),
        out_shape=jax.ShapeDtypeStruct((B, S, HEADS * V_HEAD), BF16),
        scratch_shapes=[pltpu.VMEM((HEADS_PER_STEP, 1, tq), F32),
                        pltpu.VMEM((HEADS_PER_STEP, HEAD_PAD, tq), F32),
                        pltpu.VMEM((2, HEADS_PER_STEP, TK, tq), F32),
                        pltpu.VMEM((2, HEADS_PER_STEP, 1, tq), F32)],
        compiler_params=pltpu.CompilerParams(
            dimension_semantics=("arbitrary", "arbitrary", "arbitrary"),
            vmem_limit_bytes=VMEM_LIMIT),
        name="attention",
    )(q, k, vt)


def _ffn_body(x1, h, first, wup_ref, cw_ref, cb_ref, wdn_ref, carry_ref, side_work={}):
    tm = x1.shape[0]
    fc = FFN_CHUNK

    @pl.when(first)
    def _():
        carry_ref[...] = jnp.zeros(carry_ref.shape, F32)

    first_row = lax.broadcasted_iota(jnp.int32, (8, fc), 0) == 0

    def shift_rows(t, carry_row):
        rolled = pltpu.roll(t, 1, 0)
        head = jnp.where(first_row, carry_row, rolled[:8])
        return jnp.concatenate([head, rolled[8:]], axis=0)

    def up(c):
        return (jnp.dot(h, wup_ref[:, c * fc:(c + 1) * fc], preferred_element_type=F32),
                jnp.dot(h, wup_ref[:, D_FF + c * fc:D_FF + (c + 1) * fc],
                        preferred_element_type=F32))

    def gate(c, u, v, pin):
        cols = slice(c * fc, (c + 1) * fc)
        w = 0.5 * cw_ref[:, cols]
        bias = 0.5 * cb_ref[:, cols]
        if pin is not None:
            bias = bias + pin
        t0 = u * w[0:1]
        t1 = u * w[1:2] + shift_rows(t0, carry_ref[0:1, cols])
        hc = u * w[2:3] + shift_rows(t1, carry_ref[1:2, cols]) + bias
        carry_ref[0:1, cols] = t0[tm - 1:tm]
        carry_ref[1:2, cols] = t1[tm - 1:tm]
        return ((hc + hc * jnp.tanh(hc)) * v).astype(BF16)

    acc = x1
    nxt = up(0)
    pins = {}
    for c in range(D_FF // fc):
        u, v = nxt
        if c + 1 < D_FF // fc:
            nxt = up(c + 1)
        if c in side_work:
            pins[c + 1] = side_work[c]()
        act = gate(c, u, v, pins.get(c))
        acc = acc + jnp.dot(act, wdn_ref[c * fc:(c + 1) * fc, :], preferred_element_type=F32)
    return acc


def _zero_row(v):
    half_word = jnp.uint32(16)
    bits = lax.bitcast_convert_type(v, jnp.uint32)
    return lax.shift_right_logical(lax.shift_right_logical(bits, half_word), half_word).astype(F32)


def _ffn0_kernel(x_ref, o_ref, u_ref, cw31_ref, cb31_ref, lg_ref, lb_ref, wout_ref, ng_ref,
                 wup_ref, cw_ref, cb_ref, wdn_ref, out_ref, carry_ref, x1_ref, h_ref,
                 cbuf_ref, uact_ref):
    tm = out_ref.shape[1]
    i = pl.program_id(1)
    nblk = tm // CONV_ROWS
    nchunk = D_FF // FFN_CHUNK

    def conv_load():
        cbuf_ref[0, CONV_HALO:CONV_HALO + tm, :] = u_ref[0]
        return _zero_row(_conv_shifted_copies(cbuf_ref)[0:1, 0:FFN_CHUNK])

    def conv_block(k):
        blk = _conv_rows(cbuf_ref, k * CONV_ROWS, cw31_ref, cb31_ref, lg_ref, lb_ref)
        uact_ref[k * CONV_ROWS:(k + 1) * CONV_ROWS, :] = blk
        if k == nblk - 1:
            cbuf_ref[0, 0:CONV_HALO, :] = cbuf_ref[0, tm:tm + CONV_HALO, :]
        return _zero_row(blk[0:1, 0:FFN_CHUNK].astype(F32))

    def mix(slot):
        m = jnp.concatenate([uact_ref[...], o_ref[0]], axis=1)
        x1 = x_ref[0] + jnp.dot(m, wout_ref[...], preferred_element_type=F32)
        x1_ref[slot] = x1
        h_ref[slot] = _rms(x1, ng_ref[...]).astype(BF16)

    @pl.when(i == 0)
    def _():
        cbuf_ref[0, 0:CONV_HALO, :] = jnp.zeros((CONV_HALO, CONV_CH), F32)
        conv_load()
        for k in range(nblk):
            conv_block(k)
        mix(0)

    @pl.when(i > 0)
    def _():
        cur = i % 2
        assert nblk + 2 <= nchunk - 1
        side = {0: conv_load}
        side.update({1 + k: functools.partial(conv_block, k) for k in range(nblk)})
        side[nchunk - 2] = functools.partial(mix, cur)
        out_ref[0] = _ffn_body(x1_ref[1 - cur], h_ref[1 - cur], i == 1, wup_ref, cw_ref, cb_ref,
                               wdn_ref, carry_ref, side)


def _ffn1_kernel(x_ref, mg_ref, pw_ref, ps_ref, ng_ref, wup_ref, cw_ref, cb_ref, wdn_ref,
                 fg_ref, out_ref, carry_ref, x1_ref, h_ref, hbuf_ref, y_ref):
    tm = out_ref.shape[1]
    i = pl.program_id(1)
    ntile = pl.num_programs(1) - 1
    t = jnp.minimum(i, ntile - 1) * tm + lax.broadcasted_iota(jnp.int32, (tm, 1), 0)

    def norm_stage():
        hm = _rms(x_ref[0], mg_ref[...])
        hbuf_ref[POOL_HALO:POOL_HALO + tm, :] = hm
        return _zero_row(hm[0:1, 0:FFN_CHUNK])

    def group_stage(gi):
        w = POOL_WINDOWS[gi]
        cs = slice(gi * POOL_GROUP, (gi + 1) * POOL_GROUP)
        tot = hbuf_ref[:, cs]
        span = 1
        while span < w:
            tot = tot + pltpu.roll(tot, span, 0)
            span *= 2
        hg = hbuf_ref[POOL_HALO:POOL_HALO + tm, cs]
        cnt = jnp.minimum(t + 1, w).astype(F32)
        pooled = (tot[POOL_HALO:] / cnt - hg).astype(BF16)
        y = jnp.dot(pooled, pw_ref[gi], preferred_element_type=F32)
        y_ref[:, cs] = y
        return _zero_row(y[0:1, :])

    def out_stage(slot):
        hbuf_ref[0:POOL_HALO, :] = hbuf_ref[tm:tm + POOL_HALO, :]
        x1 = x_ref[0] + y_ref[...] * ps_ref[...]
        x1_ref[slot] = x1
        h_ref[slot] = _rms(x1, ng_ref[...]).astype(BF16)

    assert POOL_GROUP == FFN_CHUNK
    ngroup = len(POOL_WINDOWS)

    @pl.when(i == 0)
    def _():
        hbuf_ref[0:POOL_HALO, :] = jnp.zeros((POOL_HALO, D_MODEL), F32)
        norm_stage()
        for gi in range(ngroup):
            group_stage(gi)
        out_stage(0)

    @pl.when(i > 0)
    def _():
        cur = i % 2
        side = {0: norm_stage}
        side.update({2 + gi: functools.partial(group_stage, gi) for gi in range(ngroup)})
        side[3 + ngroup] = functools.partial(out_stage, cur)
        x2 = _ffn_body(x1_ref[1 - cur], h_ref[1 - cur], i == 1, wup_ref, cw_ref, cb_ref, wdn_ref,
                       carry_ref, side)
        out_ref[0] = _rms(x2, fg_ref[...])


def _ffn_call(kernel, name, acts, consts, extra_scratch, B, S):
    tm = TM_FFN
    ntile = S // tm
    cur = lambda b, i: (b, jnp.minimum(i, ntile - 1), 0)
    prev = lambda b, i: (b, jnp.maximum(i - 1, 0), 0)
    return pl.pallas_call(
        kernel,
        grid=(B, ntile + 1),
        in_specs=([pl.BlockSpec((1, tm, a.shape[-1]), cur) for a in acts]
                  + [_const_spec(c.shape) for c in consts]),
        out_specs=pl.BlockSpec((1, tm, D_MODEL), prev),
        out_shape=jax.ShapeDtypeStruct((B, S, D_MODEL), F32),
        scratch_shapes=[pltpu.VMEM((8, D_FF), F32), pltpu.VMEM((2, tm, D_MODEL), F32),
                        pltpu.VMEM((2, tm, D_MODEL), BF16)] + extra_scratch,
        compiler_params=pltpu.CompilerParams(
            dimension_semantics=("arbitrary", "arbitrary"), vmem_limit_bytes=VMEM_LIMIT),
        name=name,
    )(*acts, *consts)


def _rope_tables(positions):
    B, S = positions.shape
    half = QK_ROPE // 2
    inv_freq = 1.0 / (ROPE_THETA ** (jnp.arange(0, QK_ROPE, 2, dtype=F32) / QK_ROPE))
    freq = jnp.concatenate([inv_freq, inv_freq])
    sign = jnp.tile(jnp.concatenate([-jnp.ones((half,), F32), jnp.ones((half,), F32)]), ROPE_PACK)
    pos = positions.astype(F32).reshape(B, S // TK, ROPE_PACK, ROPE_ROWS)
    ang = (jnp.swapaxes(pos, 2, 3)[..., None] * freq).reshape(B, S // TK, ROPE_ROWS, LANES)
    return jnp.cos(ang), jnp.sin(ang) * sign


def _pad_heads(w, width, take):
    K = w.shape[0]
    w = w.reshape(K, HEADS, width)[:, :, :take]
    w = jnp.pad(w, ((0, 0), (0, 0), (0, HEAD_PAD - take)))
    return w.reshape(K, HEADS * HEAD_PAD)


def kernel(x, positions, norm_mix_e, w_in, conv_w, conv_b, conv_ln_g, conv_ln_b, q_norm_g, w_uq, kv_norm_g, w_ukv, w_out, norm_mix_o, pool_w, pool_scale, norm_ffn, w_up, ffn_conv_w, ffn_conv_b, w_down, final_norm):
    B, S, _ = x.shape
    row = lambda v: v.reshape(1, -1)
    cos_t, sin_t = _rope_tables(positions)

    c0 = 2 * CONV_CH + Q_LORA + KV_LORA
    kr_cols = jnp.pad(w_in[0][:, c0:], ((0, 0), (QK_NOPE, HEAD_PAD - QK_NOPE - QK_ROPE)))
    w_in_p = jnp.concatenate([w_in[0][:, :c0], kr_cols], axis=1).astype(BF16)
    w_uq_p = _pad_heads(w_uq[0], QK_NOPE + QK_ROPE, QK_NOPE + QK_ROPE).astype(BF16)
    w_k = _pad_heads(w_ukv[0], QK_NOPE + V_HEAD, QK_NOPE)
    w_v = w_ukv[0].reshape(KV_LORA, HEADS, QK_NOPE + V_HEAD)[:, :, QK_NOPE:].reshape(KV_LORA, -1)
    w_ukv_p = jnp.concatenate([w_k, w_v], axis=1).astype(BF16)

    u, q, k, vt = _front(x, cos_t, sin_t, row(norm_mix_e[0]), w_in_p, row(q_norm_g[0]),
                        w_uq_p, row(kv_norm_g[0]), w_ukv_p)
    o = _attention(q, k, vt)

    ffn_consts = lambda l: [row(norm_ffn[l]), w_up[l].astype(BF16), ffn_conv_w[l],
                            row(ffn_conv_b[l]), w_down[l].astype(BF16)]
    x = _ffn_call(_ffn0_kernel, "ffn0", [x, o, u],
                  [conv_w[0], row(conv_b[0]), row(conv_ln_g[0]), row(conv_ln_b[0]),
                   w_out[0].astype(BF16)] + ffn_consts(0),
                  [pltpu.VMEM((8, CONV_HALO + TM_FFN, CONV_CH), F32),
                   pltpu.VMEM((TM_FFN, CONV_CH), BF16)], B, S)
    x = _ffn_call(_ffn1_kernel, "ffn1", [x],
                  [row(norm_mix_o[0]), pool_w[0].astype(BF16), row(pool_scale[0])]
                  + ffn_consts(1) + [row(final_norm)],
                  [pltpu.VMEM((POOL_HALO + TM_FFN, D_MODEL), F32),
                   pltpu.VMEM((TM_FFN, D_MODEL), F32)], B, S)
    return x
```

```python
import functools
import math

import jax
import jax.numpy as jnp
from jax import lax
from jax.experimental import pallas as pl
from jax.experimental.pallas import tpu as pltpu

D_MODEL = 1024
CONV_CH = 512
CONV_WIDTH = 31
HEADS = 8
QK_NOPE = 64
QK_ROPE = 32
V_HEAD = 64
Q_LORA = 384
KV_LORA = 256
ROPE_THETA = 10000.0
POOL_WINDOWS = (2, 4, 8, 16)
POOL_GROUP = 256
D_FF = 2816
FFN_CONV = 3
EPS = 1e-6

LANES = 128
HEAD_PAD = LANES
HEADS_PER_STEP = 4
GROUP_W = HEADS_PER_STEP * HEAD_PAD
IN_COLS_PAD = 2 * CONV_CH + Q_LORA + KV_LORA + HEAD_PAD
CONV_HALO = 32
POOL_HALO = 16
FFN_CHUNK = 256
VMEM_LIMIT = 56 * 1024 * 1024

TS_FRONT = 1024
TQ = 512
TK = 256
TM_FFN = 256
ROPE_PACK = LANES // QK_ROPE
ROPE_ROWS = TK // ROPE_PACK

BF16 = jnp.bfloat16
F32 = jnp.float32


def _rms(x, g):
    ms = jnp.mean(x * x, axis=-1, keepdims=True)
    return x * lax.rsqrt(ms + EPS) * g


def _const_spec(shape):
    nd = len(shape)
    return pl.BlockSpec(shape, lambda *_: (0,) * nd, pipeline_mode=pl.Buffered(1))


def _front_kernel(x_ref, cos_ref, sin_ref, g_ref, win_ref, qg_ref, wuq_ref,
                  kvg_ref, wukv_ref, u_ref, q_ref, k_ref, vt_ref):
    nblk = x_ref.shape[1] // TK
    c0 = 2 * CONV_CH
    scale = math.log2(math.e) / math.sqrt(QK_NOPE + QK_ROPE)
    ones = jnp.ones((V_HEAD, TK), BF16)
    first_half = lax.broadcasted_iota(jnp.int32, (TK, HEAD_PAD), 1) < QK_NOPE + QK_ROPE // 2
    lane = lax.broadcasted_iota(jnp.int32, (ROPE_ROWS, HEAD_PAD), 1)
    rope_lanes = (lane >= QK_NOPE) & (lane < QK_NOPE + QK_ROPE)

    def unpack_table(packed, fill):
        pieces = []
        for g in range(ROPE_PACK):
            shift = (QK_NOPE - QK_ROPE * g) % HEAD_PAD
            moved = pltpu.roll(packed, shift, 1) if shift else packed
            pieces.append(jnp.where(rope_lanes, moved, fill))
        return jnp.concatenate(pieces, axis=0)

    def in_proj(r):
        h = _rms(x_ref[0, r * TK:(r + 1) * TK, :], g_ref[...]).astype(BF16)
        return jnp.dot(h, win_ref[...], preferred_element_type=F32)

    def latent_proj(r, proj):
        rows = slice(r * TK, (r + 1) * TK)
        u_ref[0, rows, :] = proj[:, :CONV_CH] * jax.nn.sigmoid(proj[:, CONV_CH:c0])
        cq = _rms(proj[:, c0:c0 + Q_LORA], qg_ref[...]).astype(BF16)
        ckv = _rms(proj[:, c0 + Q_LORA:c0 + Q_LORA + KV_LORA], kvg_ref[...]).astype(BF16)
        q = jnp.dot(cq, wuq_ref[...], preferred_element_type=F32)
        kv = jnp.dot(ckv, wukv_ref[...], preferred_element_type=F32)
        return q, kv, proj[:, c0 + Q_LORA + KV_LORA:]

    def finish(r, q, kv, krb):
        rows = slice(r * TK, (r + 1) * TK)
        cos, sin = unpack_table(cos_ref[0, r], 1.0), unpack_table(sin_ref[0, r], 0.0)

        def rope(t):
            partner = jnp.where(first_half, pltpu.roll(t, HEAD_PAD - QK_ROPE // 2, 1),
                                pltpu.roll(t, QK_ROPE // 2, 1))
            return t * cos + partner * sin

        kr = rope(krb)
        for hd in range(HEADS):
            sl = slice(hd * HEAD_PAD, (hd + 1) * HEAD_PAD)
            q_ref[0, rows, sl] = (rope(q[:, sl]) * scale).astype(BF16)
            k_ref[0, rows, sl] = (kv[:, sl] + kr).astype(BF16)
        vt = kv[:, HEADS * HEAD_PAD:].T.astype(BF16)
        for hd in range(HEADS):
            vt_ref[0, r, hd * HEAD_PAD:hd * HEAD_PAD + V_HEAD, :] = vt[hd * V_HEAD:(hd + 1) * V_HEAD]
            vt_ref[0, r, hd * HEAD_PAD + V_HEAD:(hd + 1) * HEAD_PAD, :] = ones

    proj = in_proj(0)
    lat = None
    for r in range(nblk):
        nxt = in_proj(r + 1) if r + 1 < nblk else None
        if lat is not None:
            finish(r - 1, *lat)
        lat = latent_proj(r, proj)
        proj = nxt
    finish(nblk - 1, *lat)


def _front(x, cos_t, sin_t, g, w_in_p, qg, w_uq_p, kvg, w_ukv_p):
    B, S, _ = x.shape
    ts = TS_FRONT
    tile = lambda w: pl.BlockSpec((1, ts, w), lambda b, i: (b, i, 0))
    table = pl.BlockSpec((1, ts // TK, ROPE_ROWS, HEAD_PAD), lambda b, i: (b, i, 0, 0))
    return pl.pallas_call(
        _front_kernel,
        grid=(B, S // ts),
        in_specs=[tile(D_MODEL), table, table,
                  _const_spec(g.shape), _const_spec(w_in_p.shape), _const_spec(qg.shape),
                  _const_spec(w_uq_p.shape), _const_spec(kvg.shape), _const_spec(w_ukv_p.shape)],
        out_specs=[tile(CONV_CH), tile(HEADS * HEAD_PAD), tile(HEADS * HEAD_PAD),
                   pl.BlockSpec((1, ts // TK, HEADS * HEAD_PAD, TK), lambda b, i: (b, i, 0, 0))],
        out_shape=[jax.ShapeDtypeStruct((B, S, CONV_CH), F32),
                   jax.ShapeDtypeStruct((B, S, HEADS * HEAD_PAD), BF16),
                   jax.ShapeDtypeStruct((B, S, HEADS * HEAD_PAD), BF16),
                   jax.ShapeDtypeStruct((B, S // TK, HEADS * HEAD_PAD, TK), BF16)],
        compiler_params=pltpu.CompilerParams(
            dimension_semantics=("arbitrary", "arbitrary"), vmem_limit_bytes=VMEM_LIMIT),
        name="front",
    )(x, cos_t, sin_t, g, w_in_p, qg, w_uq_p, kvg, w_ukv_p)


CONV_ROWS = 32


def _conv_shifted_copies(cbuf_ref):
    buf = cbuf_ref[0]
    n = buf.shape[0]
    for r in range(1, 8):
        moved = pltpu.roll(buf, n - r, 0)
        cbuf_ref[r] = moved
    return moved


def _conv_rows(cbuf_ref, r0, w_ref, b_ref, lg_ref, lb_ref):
    base = CONV_HALO - (CONV_WIDTH - 1)
    acc = b_ref[...]
    for k in range(CONV_WIDTH):
        a, r = divmod(k + base, 8)
        acc = acc + cbuf_ref[r, r0 + 8 * a:r0 + 8 * a + CONV_ROWS, :] * w_ref[k:k + 1, :]
    mu = jnp.mean(acc, axis=-1, keepdims=True)
    d = acc - mu
    var = jnp.mean(d * d, axis=-1, keepdims=True)
    hy = d * (lax.rsqrt(var + EPS) * 0.5) * lg_ref[...] + 0.5 * lb_ref[...]
    return (hy + hy * jnp.tanh(hy)).astype(BF16)


def _attn_kernel(q_ref, k_ref, vt_ref, o_ref, m_ref, acc_ref, s_ref, cm_ref):
    tq = q_ref.shape[1]
    i = pl.program_id(2)
    neg = jnp.finfo(F32).min
    m_ref[...] = jnp.full(m_ref.shape, neg, F32)
    acc_ref[...] = jnp.zeros(acc_ref.shape, F32)
    ndiag = tq // TK
    nfull = i * ndiag

    def scores(j, slot, diag):
        koff = pl.multiple_of(j * TK, TK)
        for hh in range(HEADS_PER_STEP):
            qh = q_ref[0, :, hh * HEAD_PAD:(hh + 1) * HEAD_PAD]
            kc = k_ref[0, pl.ds(koff, TK), hh * HEAD_PAD:(hh + 1) * HEAD_PAD]
            st = lax.dot_general(kc, qh, (((1,), (1,)), ((), ())),
                                 preferred_element_type=F32)
            if diag is not None:
                key = lax.broadcasted_iota(jnp.int32, (TK, tq), 0) + diag * TK
                qry = lax.broadcasted_iota(jnp.int32, (TK, tq), 1)
                st = jnp.where(key <= qry, st, neg)
            s_ref[slot, hh] = st
            cm_ref[slot, hh] = jnp.max(st, axis=0, keepdims=True)

    def update(j, slot):
        for hh in range(HEADS_PER_STEP):
            vt = vt_ref[0, j, hh * HEAD_PAD:(hh + 1) * HEAD_PAD, :]
            m_old = m_ref[hh]
            m_new = jnp.maximum(m_old, cm_ref[slot, hh])
            alpha = jnp.exp2(m_old - m_new)
            p = jnp.exp2(s_ref[slot, hh] - m_new)
            pv = jnp.dot(vt, p.astype(BF16), preferred_element_type=F32)
            acc_ref[hh] = alpha * acc_ref[hh] + pv
            m_ref[hh] = m_new

    assert ndiag == 2
    scores(nfull, 0, 0)
    scores(nfull + 1, 1, 1)
    update(nfull, 0)

    def two_chunks(j):
        scores(j, 0, None)
        update(jnp.where(j == 0, nfull + 1, j - 1), 1)
        scores(j + 1, 1, None)
        update(j, 0)

    def four_chunks(t, c):
        two_chunks(4 * t)
        two_chunks(4 * t + 2)
        return c

    lax.fori_loop(0, i // 2, four_chunks, 0)

    @pl.when(i % 2 == 1)
    def _():
        two_chunks(nfull - 2)

    update(nfull + 1 - 2 * jnp.minimum(i, 1), 1)
    outs = [acc_ref[hh, :V_HEAD, :] / acc_ref[hh, V_HEAD:V_HEAD + 1, :]
            for hh in range(HEADS_PER_STEP)]
    o_ref[0] = jnp.concatenate(outs, axis=0).T.astype(BF16)


def _attention(q, k, vt):
    B, S, _ = q.shape
    tq = TQ
    ngroup = HEADS // HEADS_PER_STEP
    return pl.pallas_call(
        _attn_kernel,
        grid=(B, ngroup, S // tq),
        in_specs=[pl.BlockSpec((1, tq, GROUP_W), lambda b, p, i: (b, i, p)),
                  pl.BlockSpec((1, S, GROUP_W), lambda b, p, i: (b, 0, p)),
                  pl.BlockSpec((1, S // TK, GROUP_W, TK), lambda b, p, i: (b, 0, p, 0))],
        out_specs=pl.BlockSpec((1, tq, HEADS_PER_STEP * V_HEAD), lambda b, p, i: (b, i, p)),
        out_shape=jax.ShapeDtypeStruct((B, S, HEADS * V_HEAD), BF16),
        scratch_shapes=[pltpu.VMEM((HEADS_PER_STEP, 1, tq), F32),
                        pltpu.VMEM((HEADS_PER_STEP, HEAD_PAD, tq), F32),
                        pltpu.VMEM((2, HEADS_PER_STEP, TK, tq), F32),
                        pltpu.VMEM((2, HEADS_PER_STEP, 1, tq), F32)],
        compiler_params=pltpu.CompilerParams(
            dimension_semantics=("arbitrary", "arbitrary", "arbitrary"),
            vmem_limit_bytes=VMEM_LIMIT),
        name="attention",
    )(q, k, vt)


def _ffn_body(x1, h, first, wup_ref, cw_ref, cb_ref, wdn_ref, carry_ref, side_work={}):
    tm = x1.shape[0]
    fc = FFN_CHUNK

    @pl.when(first)
    def _():
        carry_ref[...] = jnp.zeros(carry_ref.shape, F32)

    first_row = lax.broadcasted_iota(jnp.int32, (8, fc), 0) == 0

    def shift_rows(t, carry_row):
        rolled = pltpu.roll(t, 1, 0)
        head = jnp.where(first_row, carry_row, rolled[:8])
        return jnp.concatenate([head, rolled[8:]], axis=0)

    def up(c):
        return (jnp.dot(h, wup_ref[:, c * fc:(c + 1) * fc], preferred_element_type=F32),
                jnp.dot(h, wup_ref[:, D_FF + c * fc:D_FF + (c + 1) * fc],
                        preferred_element_type=F32))

    def gate(c, u, v, pin):
        cols = slice(c * fc, (c + 1) * fc)
        w = 0.5 * cw_ref[:, cols]
        bias = 0.5 * cb_ref[:, cols]
        if pin is not None:
            bias = bias + pin
        t0 = u * w[0:1]
        t1 = u * w[1:2] + shift_rows(t0, carry_ref[0:1, cols])
        hc = u * w[2:3] + shift_rows(t1, carry_ref[1:2, cols]) + bias
        carry_ref[0:1, cols] = t0[tm - 1:tm]
        carry_ref[1:2, cols] = t1[tm - 1:tm]
        return ((hc + hc * jnp.tanh(hc)) * v).astype(BF16)

    acc = x1
    nxt = up(0)
    pins = {}
    for c in range(D_FF // fc):
        u, v = nxt
        if c + 1 < D_FF // fc:
            nxt = up(c + 1)
        if c in side_work:
            pins[c + 1] = side_work[c]()
        act = gate(c, u, v, pins.get(c))
        acc = acc + jnp.dot(act, wdn_ref[c * fc:(c + 1) * fc, :], preferred_element_type=F32)
    return acc


def _zero_row(v):
    half_word = jnp.uint32(16)
    bits = lax.bitcast_convert_type(v, jnp.uint32)
    return lax.shift_right_logical(lax.shift_right_logical(bits, half_word), half_word).astype(F32)


def _ffn0_kernel(x_ref, o_ref, u_ref, cw31_ref, cb31_ref, lg_ref, lb_ref, wout_ref, ng_ref,
                 wup_ref, cw_ref, cb_ref, wdn_ref, out_ref, carry_ref, x1_ref, h_ref,
                 cbuf_ref, uact_ref):
    tm = out_ref.shape[1]
    i = pl.program_id(1)
    nblk = tm // CONV_ROWS
    nchunk = D_FF // FFN_CHUNK

    def conv_load():
        cbuf_ref[0, CONV_HALO:CONV_HALO + tm, :] = u_ref[0]
        return _zero_row(_conv_shifted_copies(cbuf_ref)[0:1, 0:FFN_CHUNK])

    def conv_block(k):
        blk = _conv_rows(cbuf_ref, k * CONV_ROWS, cw31_ref, cb31_ref, lg_ref, lb_ref)
        uact_ref[k * CONV_ROWS:(k + 1) * CONV_ROWS, :] = blk
        if k == nblk - 1:
            cbuf_ref[0, 0:CONV_HALO, :] = cbuf_ref[0, tm:tm + CONV_HALO, :]
        return _zero_row(blk[0:1, 0:FFN_CHUNK].astype(F32))

    def mix(slot):
        m = jnp.concatenate([uact_ref[...], o_ref[0]], axis=1)
        x1 = x_ref[0] + jnp.dot(m, wout_ref[...], preferred_element_type=F32)
        x1_ref[slot] = x1
        h_ref[slot] = _rms(x1, ng_ref[...]).astype(BF16)

    @pl.when(i == 0)
    def _():
        cbuf_ref[0, 0:CONV_HALO, :] = jnp.zeros((CONV_HALO, CONV_CH), F32)
        conv_load()
        for k in range(nblk):
            conv_block(k)
        mix(0)

    @pl.when(i > 0)
    def _():
        cur = i % 2
        assert nblk + 2 <= nchunk - 1
        side = {0: conv_load}
        side.update({1 + k: functools.partial(conv_block, k) for k in range(nblk)})
        side[nchunk - 1] = functools.partial(mix, cur)
        out_ref[0] = _ffn_body(x1_ref[1 - cur], h_ref[1 - cur], i == 1, wup_ref, cw_ref, cb_ref,
                               wdn_ref, carry_ref, side)


def _ffn1_kernel(x_ref, mg_ref, pw_ref, ps_ref, ng_ref, wup_ref, cw_ref, cb_ref, wdn_ref,
                 fg_ref, out_ref, carry_ref, x1_ref, h_ref, hbuf_ref, y_ref):
    tm = out_ref.shape[1]
    i = pl.program_id(1)
    ntile = pl.num_programs(1) - 1
    t = jnp.minimum(i, ntile - 1) * tm + lax.broadcasted_iota(jnp.int32, (tm, 1), 0)

    def norm_stage():
        hm = _rms(x_ref[0], mg_ref[...])
        hbuf_ref[POOL_HALO:POOL_HALO + tm, :] = hm
        return _zero_row(hm[0:1, 0:FFN_CHUNK])

    def group_stage(gi):
        w = POOL_WINDOWS[gi]
        cs = slice(gi * POOL_GROUP, (gi + 1) * POOL_GROUP)
        tot = hbuf_ref[:, cs]
        span = 1
        while span < w:
            tot = tot + pltpu.roll(tot, span, 0)
            span *= 2
        hg = hbuf_ref[POOL_HALO:POOL_HALO + tm, cs]
        cnt = jnp.minimum(t + 1, w).astype(F32)
        pooled = (tot[POOL_HALO:] / cnt - hg).astype(BF16)
        y = jnp.dot(pooled, pw_ref[gi], preferred_element_type=F32)
        y_ref[:, cs] = y
        return _zero_row(y[0:1, :])

    def out_stage(slot):
        hbuf_ref[0:POOL_HALO, :] = hbuf_ref[tm:tm + POOL_HALO, :]
        x1 = x_ref[0] + y_ref[...] * ps_ref[...]
        x1_ref[slot] = x1
        h_ref[slot] = _rms(x1, ng_ref[...]).astype(BF16)

    assert POOL_GROUP == FFN_CHUNK
    ngroup = len(POOL_WINDOWS)

    @pl.when(i == 0)
    def _():
        hbuf_ref[0:POOL_HALO, :] = jnp.zeros((POOL_HALO, D_MODEL), F32)
        norm_stage()
        for gi in range(ngroup):
            group_stage(gi)
        out_stage(0)

    @pl.when(i > 0)
    def _():
        cur = i % 2
        side = {0: norm_stage}
        side.update({2 + gi: functools.partial(group_stage, gi) for gi in range(ngroup)})
        side[3 + ngroup] = functools.partial(out_stage, cur)
        x2 = _ffn_body(x1_ref[1 - cur], h_ref[1 - cur], i == 1, wup_ref, cw_ref, cb_ref, wdn_ref,
                       carry_ref, side)
        out_ref[0] = _rms(x2, fg_ref[...])


def _ffn_call(kernel, name, acts, consts, extra_scratch, B, S):
    tm = TM_FFN
    ntile = S // tm
    cur = lambda b, i: (b, jnp.minimum(i, ntile - 1), 0)
    prev = lambda b, i: (b, jnp.maximum(i - 1, 0), 0)
    return pl.pallas_call(
        kernel,
        grid=(B, ntile + 1),
        in_specs=([pl.BlockSpec((1, tm, a.shape[-1]), cur) for a in acts]
                  + [_const_spec(c.shape) for c in consts]),
        out_specs=pl.BlockSpec((1, tm, D_MODEL), prev),
        out_shape=jax.ShapeDtypeStruct((B, S, D_MODEL), F32),
        scratch_shapes=[pltpu.VMEM((8, D_FF), F32), pltpu.VMEM((2, tm, D_MODEL), F32),
                        pltpu.VMEM((2, tm, D_MODEL), BF16)] + extra_scratch,
        compiler_params=pltpu.CompilerParams(
            dimension_semantics=("arbitrary", "arbitrary"), vmem_limit_bytes=VMEM_LIMIT),
        name=name,
    )(*acts, *consts)


def _rope_tables(positions):
    B, S = positions.shape
    half = QK_ROPE // 2
    inv_freq = 1.0 / (ROPE_THETA ** (jnp.arange(0, QK_ROPE, 2, dtype=F32) / QK_ROPE))
    freq = jnp.concatenate([inv_freq, inv_freq])
    sign = jnp.tile(jnp.concatenate([-jnp.ones((half,), F32), jnp.ones((half,), F32)]), ROPE_PACK)
    pos = positions.astype(F32).reshape(B, S // TK, ROPE_PACK, ROPE_ROWS)
    ang = (jnp.swapaxes(pos, 2, 3)[..., None] * freq).reshape(B, S // TK, ROPE_ROWS, LANES)
    return jnp.cos(ang), jnp.sin(ang) * sign


def _pad_heads(w, width, take):
    K = w.shape[0]
    w = w.reshape(K, HEADS, width)[:, :, :take]
    w = jnp.pad(w, ((0, 0), (0, 0), (0, HEAD_PAD - take)))
    return w.reshape(K, HEADS * HEAD_PAD)


def kernel(x, positions, norm_mix_e, w_in, conv_w, conv_b, conv_ln_g, conv_ln_b, q_norm_g, w_uq, kv_norm_g, w_ukv, w_out, norm_mix_o, pool_w, pool_scale, norm_ffn, w_up, ffn_conv_w, ffn_conv_b, w_down, final_norm):
    B, S, _ = x.shape
    row = lambda v: v.reshape(1, -1)
    cos_t, sin_t = _rope_tables(positions)

    c0 = 2 * CONV_CH + Q_LORA + KV_LORA
    kr_cols = jnp.pad(w_in[0][:, c0:], ((0, 0), (QK_NOPE, HEAD_PAD - QK_NOPE - QK_ROPE)))
    w_in_p = jnp.concatenate([w_in[0][:, :c0], kr_cols], axis=1).astype(BF16)
    w_uq_p = _pad_heads(w_uq[0], QK_NOPE + QK_ROPE, QK_NOPE + QK_ROPE).astype(BF16)
    w_k = _pad_heads(w_ukv[0], QK_NOPE + V_HEAD, QK_NOPE)
    w_v = w_ukv[0].reshape(KV_LORA, HEADS, QK_NOPE + V_HEAD)[:, :, QK_NOPE:].reshape(KV_LORA, -1)
    w_ukv_p = jnp.concatenate([w_k, w_v], axis=1).astype(BF16)

    u, q, k, vt = _front(x, cos_t, sin_t, row(norm_mix_e[0]), w_in_p, row(q_norm_g[0]),
                        w_uq_p, row(kv_norm_g[0]), w_ukv_p)
    o = _attention(q, k, vt)

    ffn_consts = lambda l: [row(norm_ffn[l]), w_up[l].astype(BF16), ffn_conv_w[l],
                            row(ffn_conv_b[l]), w_down[l].astype(BF16)]
    x = _ffn_call(_ffn0_kernel, "ffn0", [x, o, u],
                  [conv_w[0], row(conv_b[0]), row(conv_ln_g[0]), row(conv_ln_b[0]),
                   w_out[0].astype(BF16)] + ffn_consts(0),
                  [pltpu.VMEM((8, CONV_HALO + TM_FFN, CONV_CH), F32),
                   pltpu.VMEM((TM_FFN, CONV_CH), BF16)], B, S)
    x = _ffn_call(_ffn1_kernel, "ffn1", [x],
                  [row(norm_mix_o[0]), pool_w[0].astype(BF16), row(pool_scale[0])]
                  + ffn_consts(1) + [row(final_norm)],
                  [pltpu.VMEM((POOL_HALO + TM_FFN, D_MODEL), F32),
                   pltpu.VMEM((TM_FFN, D_MODEL), F32)], B, S)
    return x
```

```python
import functools
import math

import jax
import jax.numpy as jnp
from jax import lax
from jax.experimental import pallas as pl
from jax.experimental.pallas import tpu as pltpu

D_MODEL = 1024
CONV_CH = 512
CONV_WIDTH = 31
HEADS = 8
QK_NOPE = 64
QK_ROPE = 32
V_HEAD = 64
Q_LORA = 384
KV_LORA = 256
ROPE_THETA = 10000.0
POOL_WINDOWS = (2, 4, 8, 16)
POOL_GROUP = 256
D_FF = 2816
FFN_CONV = 3
EPS = 1e-6

LANES = 128
HEAD_PAD = LANES
HEADS_PER_STEP = 4
GROUP_W = HEADS_PER_STEP * HEAD_PAD
IN_COLS_PAD = 2 * CONV_CH + Q_LORA + KV_LORA + HEAD_PAD
CONV_HALO = 32
POOL_HALO = 16
FFN_CHUNK = 256
VMEM_LIMIT = 56 * 1024 * 1024

TS_FRONT = 1024
TQ = 512
TK = 256
TM_FFN = 256
ROPE_PACK = LANES // QK_ROPE
ROPE_ROWS = TK // ROPE_PACK

BF16 = jnp.bfloat16
F32 = jnp.float32


def _rms(x, g):
    ms = jnp.mean(x * x, axis=-1, keepdims=True)
    return x * lax.rsqrt(ms + EPS) * g


def _const_spec(shape):
    nd = len(shape)
    return pl.BlockSpec(shape, lambda *_: (0,) * nd, pipeline_mode=pl.Buffered(1))


def _front_kernel(x_ref, cos_ref, sin_ref, g_ref, win_ref, qg_ref, wuq_ref,
                  kvg_ref, wukv_ref, u_ref, q_ref, k_ref, vt_ref):
    nblk = x_ref.shape[1] // TK
    c0 = 2 * CONV_CH
    scale = math.log2(math.e) / math.sqrt(QK_NOPE + QK_ROPE)
    ones = jnp.ones((V_HEAD, TK), BF16)
    first_half = lax.broadcasted_iota(jnp.int32, (TK, HEAD_PAD), 1) < QK_NOPE + QK_ROPE // 2
    lane = lax.broadcasted_iota(jnp.int32, (ROPE_ROWS, HEAD_PAD), 1)
    rope_lanes = (lane >= QK_NOPE) & (lane < QK_NOPE + QK_ROPE)

    def unpack_table(packed, fill):
        pieces = []
        for g in range(ROPE_PACK):
            shift = (QK_NOPE - QK_ROPE * g) % HEAD_PAD
            moved = pltpu.roll(packed, shift, 1) if shift else packed
            pieces.append(jnp.where(rope_lanes, moved, fill))
        return jnp.concatenate(pieces, axis=0)

    def in_proj(r):
        h = _rms(x_ref[0, r * TK:(r + 1) * TK, :], g_ref[...]).astype(BF16)
        return jnp.dot(h, win_ref[...], preferred_element_type=F32)

    def latent_proj(r, proj):
        rows = slice(r * TK, (r + 1) * TK)
        u_ref[0, rows, :] = proj[:, :CONV_CH] * jax.nn.sigmoid(proj[:, CONV_CH:c0])
        cq = _rms(proj[:, c0:c0 + Q_LORA], qg_ref[...]).astype(BF16)
        ckv = _rms(proj[:, c0 + Q_LORA:c0 + Q_LORA + KV_LORA], kvg_ref[...]).astype(BF16)
        q = jnp.dot(cq, wuq_ref[...], preferred_element_type=F32)
        kv = jnp.dot(ckv, wukv_ref[...], preferred_element_type=F32)
        return q, kv, proj[:, c0 + Q_LORA + KV_LORA:]

    def finish(r, q, kv, krb):
        rows = slice(r * TK, (r + 1) * TK)
        cos, sin = unpack_table(cos_ref[0, r], 1.0), unpack_table(sin_ref[0, r], 0.0)

        def rope(t):
            partner = jnp.where(first_half, pltpu.roll(t, HEAD_PAD - QK_ROPE // 2, 1),
                                pltpu.roll(t, QK_ROPE // 2, 1))
            return t * cos + partner * sin

        kr = rope(krb)
        for hd in range(HEADS):
            sl = slice(hd * HEAD_PAD, (hd + 1) * HEAD_PAD)
            q_ref[0, rows, sl] = (rope(q[:, sl]) * scale).astype(BF16)
            k_ref[0, rows, sl] = (kv[:, sl] + kr).astype(BF16)
        vt = kv[:, HEADS * HEAD_PAD:].T.astype(BF16)
        for hd in range(HEADS):
            vt_ref[0, r, hd * HEAD_PAD:hd * HEAD_PAD + V_HEAD, :] = vt[hd * V_HEAD:(hd + 1) * V_HEAD]
            vt_ref[0, r, hd * HEAD_PAD + V_HEAD:(hd + 1) * HEAD_PAD, :] = ones

    proj = in_proj(0)
    lat = None
    for r in range(nblk):
        nxt = in_proj(r + 1) if r + 1 < nblk else None
        if lat is not None:
            finish(r - 1, *lat)
        lat = latent_proj(r, proj)
        proj = nxt
    finish(nblk - 1, *lat)


def _front(x, cos_t, sin_t, g, w_in_p, qg, w_uq_p, kvg, w_ukv_p):
    B, S, _ = x.shape
    ts = TS_FRONT
    tile = lambda w: pl.BlockSpec((1, ts, w), lambda b, i: (b, i, 0))
    table = pl.BlockSpec((1, ts // TK, ROPE_ROWS, HEAD_PAD), lambda b, i: (b, i, 0, 0))
    return pl.pallas_call(
        _front_kernel,
        grid=(B, S // ts),
        in_specs=[tile(D_MODEL), table, table,
                  _const_spec(g.shape), _const_spec(w_in_p.shape), _const_spec(qg.shape),
                  _const_spec(w_uq_p.shape), _const_spec(kvg.shape), _const_spec(w_ukv_p.shape)],
        out_specs=[tile(CONV_CH), tile(HEADS * HEAD_PAD), tile(HEADS * HEAD_PAD),
                   pl.BlockSpec((1, ts // TK, HEADS * HEAD_PAD, TK), lambda b, i: (b, i, 0, 0))],
        out_shape=[jax.ShapeDtypeStruct((B, S, CONV_CH), F32),
                   jax.ShapeDtypeStruct((B, S, HEADS * HEAD_PAD), BF16),
                   jax.ShapeDtypeStruct((B, S, HEADS * HEAD_PAD), BF16),
                   jax.ShapeDtypeStruct((B, S // TK, HEADS * HEAD_PAD, TK), BF16)],
        compiler_params=pltpu.CompilerParams(
            dimension_semantics=("arbitrary", "arbitrary"), vmem_limit_bytes=VMEM_LIMIT),
        name="front",
    )(x, cos_t, sin_t, g, w_in_p, qg, w_uq_p, kvg, w_ukv_p)


CONV_ROWS = 32


def _conv_shifted_copies(cbuf_ref):
    buf = cbuf_ref[0]
    n = buf.shape[0]
    for r in range(1, 8):
        moved = pltpu.roll(buf, n - r, 0)
        cbuf_ref[r] = moved
    return moved


def _conv_rows(cbuf_ref, r0, w_ref, b_ref, lg_ref, lb_ref):
    base = CONV_HALO - (CONV_WIDTH - 1)
    acc = b_ref[...]
    for k in range(CONV_WIDTH):
        a, r = divmod(k + base, 8)
        acc = acc + cbuf_ref[r, r0 + 8 * a:r0 + 8 * a + CONV_ROWS, :] * w_ref[k:k + 1, :]
    mu = jnp.mean(acc, axis=-1, keepdims=True)
    d = acc - mu
    var = jnp.mean(d * d, axis=-1, keepdims=True)
    hy = d * (lax.rsqrt(var + EPS) * 0.5) * lg_ref[...] + 0.5 * lb_ref[...]
    return (hy + hy * jnp.tanh(hy)).astype(BF16)


def _attn_kernel(q_ref, k_ref, vt_ref, o_ref, m_ref, acc_ref, s_ref, cm_ref):
    tq = q_ref.shape[1]
    i = pl.program_id(2)
    neg = jnp.finfo(F32).min
    m_ref[...] = jnp.full(m_ref.shape, neg, F32)
    acc_ref[...] = jnp.zeros(acc_ref.shape, F32)
    ndiag = tq // TK
    nfull = i * ndiag

    def scores(j, slot, diag):
        koff = pl.multiple_of(j * TK, TK)
        for hh in range(HEADS_PER_STEP):
            qh = q_ref[0, :, hh * HEAD_PAD:(hh + 1) * HEAD_PAD]
            kc = k_ref[0, pl.ds(koff, TK), hh * HEAD_PAD:(hh + 1) * HEAD_PAD]
            st = lax.dot_general(kc, qh, (((1,), (1,)), ((), ())),
                                 preferred_element_type=F32)
            if diag is not None:
                key = lax.broadcasted_iota(jnp.int32, (TK, tq), 0) + diag * TK
                qry = lax.broadcasted_iota(jnp.int32, (TK, tq), 1)
                st = jnp.where(key <= qry, st, neg)
            s_ref[slot, hh] = st
            cm_ref[slot, hh] = jnp.max(st, axis=0, keepdims=True)

    def update(j, slot):
        for hh in range(HEADS_PER_STEP):
            vt = vt_ref[0, j, hh * HEAD_PAD:(hh + 1) * HEAD_PAD, :]
            m_old = m_ref[hh]
            m_new = jnp.maximum(m_old, cm_ref[slot, hh])
            alpha = jnp.exp2(m_old - m_new)
            p = jnp.exp2(s_ref[slot, hh] - m_new)
            pv = jnp.dot(vt, p.astype(BF16), preferred_element_type=F32)
            acc_ref[hh] = alpha * acc_ref[hh] + pv
            m_ref[hh] = m_new

    assert ndiag == 2
    scores(nfull, 0, 0)
    scores(nfull + 1, 1, 1)
    update(nfull, 0)

    def two_chunks(j):
        scores(j, 0, None)
        update(jnp.where(j == 0, nfull + 1, j - 1), 1)
        scores(j + 1, 1, None)
        update(j, 0)

    def four_chunks(t, c):
        two_chunks(4 * t)
        two_chunks(4 * t + 2)
        return c

    lax.fori_loop(0, i // 2, four_chunks, 0)

    @pl.when(i % 2 == 1)
    def _():
        two_chunks(nfull - 2)

    update(nfull + 1 - 2 * jnp.minimum(i, 1), 1)
    outs = [acc_ref[hh, :V_HEAD, :] / acc_ref[hh, V_HEAD:V_HEAD + 1, :]
            for hh in range(HEADS_PER_STEP)]
    o_ref[0] = jnp.concatenate(outs, axis=0).T.astype(BF16)


def _attention(q, k, vt):
    B, S, _ = q.shape
    tq = TQ
    ngroup = HEADS // HEADS_PER_STEP
    return pl.pallas_call(
        _attn_kernel,
        grid=(B, ngroup, S // tq),
        in_specs=[pl.BlockSpec((1, tq, GROUP_W), lambda b, p, i: (b, i, p)),
                  pl.BlockSpec((1, S, GROUP_W), lambda b, p, i: (b, 0, p)),
                  pl.BlockSpec((1, S // TK, GROUP_W, TK), lambda b, p, i: (b, 0, p, 0))],
        out_specs=pl.BlockSpec((1, tq, HEADS_PER_STEP * V_HEAD), lambda b, p, i: (b, i, p)),
        out_shape=jax.ShapeDtypeStruct((B, S, HEADS * V_HEAD), BF16),
        scratch_shapes=[pltpu.VMEM((HEADS_PER_STEP, 1, tq), F32),
                        pltpu.VMEM((HEADS_PER_STEP, HEAD_PAD, tq), F32),
                        pltpu.VMEM((2, HEADS_PER_STEP, TK, tq), F32),
                        pltpu.VMEM((2, HEADS_PER_STEP, 1, tq), F32)],
        compiler_params=pltpu.CompilerParams(
            dimension_semantics=("arbitrary", "arbitrary", "arbitrary"),
            vmem_limit_bytes=VMEM_LIMIT),
        name="attention",
    )(q, k, vt)


def _ffn_body(x1_at, h_at, first, wup_ref, cw_ref, cb_ref, wdn_ref, carry_ref, side_work={}):
    tm = x1_at.shape[0]
    fc = FFN_CHUNK

    @pl.when(first)
    def _():
        carry_ref[...] = jnp.zeros(carry_ref.shape, F32)

    first_row = lax.broadcasted_iota(jnp.int32, (8, fc), 0) == 0

    def shift_rows(t, carry_row):
        rolled = pltpu.roll(t, 1, 0)
        head = jnp.where(first_row, carry_row, rolled[:8])
        return jnp.concatenate([head, rolled[8:]], axis=0)

    def up(c):
        return (jnp.dot(h_at[...], wup_ref[:, c * fc:(c + 1) * fc], preferred_element_type=F32),
                jnp.dot(h_at[...], wup_ref[:, D_FF + c * fc:D_FF + (c + 1) * fc],
                        preferred_element_type=F32))

    def gate(c, u, v, pin):
        cols = slice(c * fc, (c + 1) * fc)
        w = 0.5 * cw_ref[:, cols]
        bias = 0.5 * cb_ref[:, cols]
        if pin is not None:
            bias = bias + pin
        t0 = u * w[0:1]
        t1 = u * w[1:2] + shift_rows(t0, carry_ref[0:1, cols])
        hc = u * w[2:3] + shift_rows(t1, carry_ref[1:2, cols]) + bias
        carry_ref[0:1, cols] = t0[tm - 1:tm]
        carry_ref[1:2, cols] = t1[tm - 1:tm]
        return ((hc + hc * jnp.tanh(hc)) * v).astype(BF16)

    acc = None
    nxt = up(0)
    pins = {}
    for c in range(D_FF // fc):
        u, v = nxt
        if c + 1 < D_FF // fc:
            nxt = up(c + 1)
        if c in side_work:
            pins[c + 1] = side_work[c]()
        act = gate(c, u, v, pins.get(c))
        down = jnp.dot(act, wdn_ref[c * fc:(c + 1) * fc, :], preferred_element_type=F32)
        acc = down if acc is None else acc + down
    return acc + x1_at[...]


def _zero_row(v):
    half_word = jnp.uint32(16)
    bits = lax.bitcast_convert_type(v, jnp.uint32)
    return lax.shift_right_logical(lax.shift_right_logical(bits, half_word), half_word).astype(F32)


def _ffn0_kernel(x_ref, o_ref, u_ref, cw31_ref, cb31_ref, lg_ref, lb_ref, wout_ref, ng_ref,
                 wup_ref, cw_ref, cb_ref, wdn_ref, out_ref, carry_ref, x1_ref, h_ref,
                 cbuf_ref, uact_ref):
    tm = out_ref.shape[1]
    i = pl.program_id(1)
    nblk = tm // CONV_ROWS
    nchunk = D_FF // FFN_CHUNK

    def conv_load():
        cbuf_ref[0, CONV_HALO:CONV_HALO + tm, :] = u_ref[0]
        return _zero_row(_conv_shifted_copies(cbuf_ref)[0:1, 0:FFN_CHUNK])

    def conv_block(k):
        blk = _conv_rows(cbuf_ref, k * CONV_ROWS, cw31_ref, cb31_ref, lg_ref, lb_ref)
        uact_ref[k * CONV_ROWS:(k + 1) * CONV_ROWS, :] = blk
        if k == nblk - 1:
            cbuf_ref[0, 0:CONV_HALO, :] = cbuf_ref[0, tm:tm + CONV_HALO, :]
        return _zero_row(blk[0:1, 0:FFN_CHUNK].astype(F32))

    def mix(slot):
        m = jnp.concatenate([uact_ref[...], o_ref[0]], axis=1)
        x1 = x_ref[0] + jnp.dot(m, wout_ref[...], preferred_element_type=F32)
        x1_ref[slot] = x1
        h_ref[slot] = _rms(x1, ng_ref[...]).astype(BF16)

    @pl.when(i == 0)
    def _():
        cbuf_ref[0, 0:CONV_HALO, :] = jnp.zeros((CONV_HALO, CONV_CH), F32)
        conv_load()
        for k in range(nblk):
            conv_block(k)
        mix(0)

    @pl.when(i > 0)
    def _():
        cur = i % 2
        assert nblk + 2 <= nchunk - 1
        side = {0: conv_load}
        side.update({1 + k: functools.partial(conv_block, k) for k in range(nblk)})
        side[nchunk - 1] = functools.partial(mix, cur)
        out_ref[0] = _ffn_body(x1_ref.at[1 - cur], h_ref.at[1 - cur], i == 1, wup_ref, cw_ref, cb_ref,
                               wdn_ref, carry_ref, side)


def _ffn1_kernel(x_ref, mg_ref, pw_ref, ps_ref, ng_ref, wup_ref, cw_ref, cb_ref, wdn_ref,
                 fg_ref, out_ref, carry_ref, x1_ref, h_ref, hbuf_ref, y_ref):
    tm = out_ref.shape[1]
    i = pl.program_id(1)
    ntile = pl.num_programs(1) - 1
    t = jnp.minimum(i, ntile - 1) * tm + lax.broadcasted_iota(jnp.int32, (tm, 1), 0)

    def norm_stage():
        hm = _rms(x_ref[0], mg_ref[...])
        hbuf_ref[POOL_HALO:POOL_HALO + tm, :] = hm
        return _zero_row(hm[0:1, 0:FFN_CHUNK])

    def group_stage(gi):
        w = POOL_WINDOWS[gi]
        cs = slice(gi * POOL_GROUP, (gi + 1) * POOL_GROUP)
        tot = hbuf_ref[:, cs]
        span = 1
        while span < w:
            tot = tot + pltpu.roll(tot, span, 0)
            span *= 2
        hg = hbuf_ref[POOL_HALO:POOL_HALO + tm, cs]
        cnt = jnp.minimum(t + 1, w).astype(F32)
        pooled = (tot[POOL_HALO:] / cnt - hg).astype(BF16)
        y = jnp.dot(pooled, pw_ref[gi], preferred_element_type=F32)
        y_ref[:, cs] = y
        return _zero_row(y[0:1, :])

    def out_stage(slot):
        hbuf_ref[0:POOL_HALO, :] = hbuf_ref[tm:tm + POOL_HALO, :]
        x1 = x_ref[0] + y_ref[...] * ps_ref[...]
        x1_ref[slot] = x1
        h_ref[slot] = _rms(x1, ng_ref[...]).astype(BF16)

    assert POOL_GROUP == FFN_CHUNK
    ngroup = len(POOL_WINDOWS)

    @pl.when(i == 0)
    def _():
        hbuf_ref[0:POOL_HALO, :] = jnp.zeros((POOL_HALO, D_MODEL), F32)
        norm_stage()
        for gi in range(ngroup):
            group_stage(gi)
        out_stage(0)

    @pl.when(i > 0)
    def _():
        cur = i % 2
        side = {0: norm_stage}
        side.update({2 + gi: functools.partial(group_stage, gi) for gi in range(ngroup)})
        side[3 + ngroup] = functools.partial(out_stage, cur)
        x2 = _ffn_body(x1_ref.at[1 - cur], h_ref.at[1 - cur], i == 1, wup_ref, cw_ref, cb_ref, wdn_ref,
                       carry_ref, side)
        out_ref[0] = _rms(x2, fg_ref[...])


def _ffn_call(kernel, name, acts, consts, extra_scratch, B, S):
    tm = TM_FFN
    ntile = S // tm
    cur = lambda b, i: (b, jnp.minimum(i, ntile - 1), 0)
    prev = lambda b, i: (b, jnp.maximum(i - 1, 0), 0)
    return pl.pallas_call(
        kernel,
        grid=(B, ntile + 1),
        in_specs=([pl.BlockSpec((1, tm, a.shape[-1]), cur) for a in acts]
                  + [_const_spec(c.shape) for c in consts]),
        out_specs=pl.BlockSpec((1, tm, D_MODEL), prev),
        out_shape=jax.ShapeDtypeStruct((B, S, D_MODEL), F32),
        scratch_shapes=[pltpu.VMEM((8, D_FF), F32), pltpu.VMEM((2, tm, D_MODEL), F32),
                        pltpu.VMEM((2, tm, D_MODEL), BF16)] + extra_scratch,
        compiler_params=pltpu.CompilerParams(
            dimension_semantics=("arbitrary", "arbitrary"), vmem_limit_bytes=VMEM_LIMIT),
        name=name,
    )(*acts, *consts)


def _rope_tables(positions):
    B, S = positions.shape
    half = QK_ROPE // 2
    inv_freq = 1.0 / (ROPE_THETA ** (jnp.arange(0, QK_ROPE, 2, dtype=F32) / QK_ROPE))
    freq = jnp.concatenate([inv_freq, inv_freq])
    sign = jnp.tile(jnp.concatenate([-jnp.ones((half,), F32), jnp.ones((half,), F32)]), ROPE_PACK)
    pos = positions.astype(F32).reshape(B, S // TK, ROPE_PACK, ROPE_ROWS)
    ang = (jnp.swapaxes(pos, 2, 3)[..., None] * freq).reshape(B, S // TK, ROPE_ROWS, LANES)
    return jnp.cos(ang), jnp.sin(ang) * sign


def _pad_heads(w, width, take):
    K = w.shape[0]
    w = w.reshape(K, HEADS, width)[:, :, :take]
    w = jnp.pad(w, ((0, 0), (0, 0), (0, HEAD_PAD - take)))
    return w.reshape(K, HEADS * HEAD_PAD)


def kernel(x, positions, norm_mix_e, w_in, conv_w, conv_b, conv_ln_g, conv_ln_b, q_norm_g, w_uq, kv_norm_g, w_ukv, w_out, norm_mix_o, pool_w, pool_scale, norm_ffn, w_up, ffn_conv_w, ffn_conv_b, w_down, final_norm):
    B, S, _ = x.shape
    row = lambda v: v.reshape(1, -1)
    cos_t, sin_t = _rope_tables(positions)

    c0 = 2 * CONV_CH + Q_LORA + KV_LORA
    kr_cols = jnp.pad(w_in[0][:, c0:], ((0, 0), (QK_NOPE, HEAD_PAD - QK_NOPE - QK_ROPE)))
    w_in_p = jnp.concatenate([w_in[0][:, :c0], kr_cols], axis=1).astype(BF16)
    w_uq_p = _pad_heads(w_uq[0], QK_NOPE + QK_ROPE, QK_NOPE + QK_ROPE).astype(BF16)
    w_k = _pad_heads(w_ukv[0], QK_NOPE + V_HEAD, QK_NOPE)
    w_v = w_ukv[0].reshape(KV_LORA, HEADS, QK_NOPE + V_HEAD)[:, :, QK_NOPE:].reshape(KV_LORA, -1)
    w_ukv_p = jnp.concatenate([w_k, w_v], axis=1).astype(BF16)

    u, q, k, vt = _front(x, cos_t, sin_t, row(norm_mix_e[0]), w_in_p, row(q_norm_g[0]),
                        w_uq_p, row(kv_norm_g[0]), w_ukv_p)
    o = _attention(q, k, vt)

    ffn_consts = lambda l: [row(norm_ffn[l]), w_up[l].astype(BF16), ffn_conv_w[l],
                            row(ffn_conv_b[l]), w_down[l].astype(BF16)]
    x = _ffn_call(_ffn0_kernel, "ffn0", [x, o, u],
                  [conv_w[0], row(conv_b[0]), row(conv_ln_g[0]), row(conv_ln_b[0]),
                   w_out[0].astype(BF16)] + ffn_consts(0),
                  [pltpu.VMEM((8, CONV_HALO + TM_FFN, CONV_CH), F32),
                   pltpu.VMEM((TM_FFN, CONV_CH), BF16)], B, S)
    x = _ffn_call(_ffn1_kernel, "ffn1", [x],
                  [row(norm_mix_o[0]), pool_w[0].astype(BF16), row(pool_scale[0])]
                  + ffn_consts(1) + [row(final_norm)],
                  [pltpu.VMEM((POOL_HALO + TM_FFN, D_MODEL), F32),
                   pltpu.VMEM((TM_FFN, D_MODEL), F32)], B, S)
    return x
```
